```python
import jax, jax.numpy as jnp
from jax import lax
import numpy as np

D_MODEL = 4096
BATCH = 4
SEQ = 2048
DEPTH = 2
DEC_BATCH = 128
DEC_SEQ = 4
PAST_LEN = 16384
PAGE_SIZE = 128

N_HEADS = 16
QK_NOPE = 128
QK_ROPE = 64
V_HEAD = 128
Q_RANK = 1024
KV_RANK = 512
D_ATTN = N_HEADS * V_HEAD
D_RNN = D_MODEL - D_ATTN
N_RNN_BLOCKS = 16
RNN_BLOCK = D_RNN // N_RNN_BLOCKS
CONV_W = 4
LRU_C = 8.0
MIX_WIDTH = D_ATTN + D_RNN
D_IN = Q_RANK + KV_RANK + QK_ROPE + 2 * D_RNN
SPLITS = (Q_RANK, Q_RANK + KV_RANK, Q_RANK + KV_RANK + QK_ROPE, Q_RANK + KV_RANK + QK_ROPE + D_RNN)
D_FF = -(-(8 * D_MODEL) // (3 * 256)) * 256
ROPE_THETA = 10000.0
EPS = 1e-6
Q_BLOCK = 128
ATTN_SCALE = (QK_NOPE + QK_ROPE) ** -0.5

kernel_name = 'hymba_mla_rglru_decoder_step'


def rmsnorm(x, g):
    xf = x.astype(jnp.float32)
    y = xf * lax.rsqrt(jnp.mean(xf * xf, axis=-1, keepdims=True) + EPS)
    return (y * g.astype(jnp.float32)).astype(x.dtype)


def rope(x, pos):
    half = QK_ROPE // 2
    inv = ROPE_THETA ** (-jnp.arange(half, dtype=jnp.float32) / half)
    ang = pos.astype(jnp.float32)[:, None] * inv[None, :]
    ang = ang.reshape((ang.shape[0],) + (1,) * (x.ndim - 3) + (half,))
    cos, sin = jnp.cos(ang), jnp.sin(ang)
    xf = x.astype(jnp.float32)
    x1, x2 = xf[..., :half], xf[..., half:]
    return jnp.concatenate([x1 * cos - x2 * sin, x2 * cos + x1 * sin], axis=-1).astype(x.dtype)


def latent_attention(q_lat, q_pe, ckv, kpe, q_pos, k_pos):
    s = (jnp.einsum('bqhl,bkl->bhqk', q_lat, ckv).astype(jnp.float32)
         + jnp.einsum('bqhr,bkr->bhqk', q_pe, kpe).astype(jnp.float32)) * ATTN_SCALE
    s = jnp.where(k_pos[None, :] <= q_pos[:, None], s, -jnp.inf)
    p = jax.nn.softmax(s, axis=-1)
    return jnp.einsum('bhqk,bkl->bqhl', p.astype(ckv.dtype), ckv)


def prompt_attend(q_lat, q_pe, ckv, kpe):
    B, T = q_lat.shape[0], q_lat.shape[1]
    nb = T // Q_BLOCK
    pos = jnp.arange(T)

    def block(args):
        ql, qp, qpos = args
        return latent_attention(ql, qp, ckv, kpe, qpos, pos)

    blocks = (q_lat.reshape(B, nb, Q_BLOCK, N_HEADS, KV_RANK).swapaxes(0, 1),
              q_pe.reshape(B, nb, Q_BLOCK, N_HEADS, QK_ROPE).swapaxes(0, 1),
              pos.reshape(nb, Q_BLOCK))
    o = lax.map(block, blocks)
    return o.swapaxes(0, 1).reshape(B, T, N_HEADS, KV_RANK)


def make_sample_attend(layer, cache_ckv, cache_kpe, page_table):
    def attend(q_lat, q_pe, ckv, kpe):
        T = q_lat.shape[1]
        q_pos = PAST_LEN + jnp.arange(T)
        k_pos = jnp.arange(PAST_LEN + T)

        def one_seq(args):
            ql, qp, ck_new, kp_new, pages = args
            ck = jnp.concatenate([cache_ckv[layer, pages].reshape(-1, KV_RANK).astype(ck_new.dtype), ck_new], axis=0)
            kp = jnp.concatenate([cache_kpe[layer, pages].reshape(-1, QK_ROPE).astype(kp_new.dtype), kp_new], axis=0)
            return latent_attention(ql[None], qp[None], ck[None], kp[None], q_pos, k_pos)[0]

        return lax.map(one_seq, (q_lat, q_pe, ckv, kpe, page_table))
    return attend


def causal_conv(x, buf, w, b):
    xp = jnp.concatenate([buf.astype(x.dtype), x], axis=1)
    T = x.shape[1]
    y = b + sum(xp[:, k:k + T] * w[k] for k in range(CONV_W))
    return y, xp[:, xp.shape[1] - (CONV_W - 1):]


def block_diag(x, w, b):
    xb = x.reshape(x.shape[:-1] + (N_RNN_BLOCKS, RNN_BLOCK))
    return jnp.einsum('btni,nij->btnj', xb, w).reshape(x.shape) + b


def rglru(x, h0, w_ra, b_ra, w_ri, b_ri, lru_lambda):
    xf = x.astype(jnp.float32)
    r = jax.nn.sigmoid(block_diag(x, w_ra, b_ra).astype(jnp.float32))
    i = jax.nn.sigmoid(block_diag(x, w_ri, b_ri).astype(jnp.float32))
    log_a = -LRU_C * r * jax.nn.softplus(-lru_lambda.astype(jnp.float32))
    a = jnp.exp(log_a)
    u = jnp.sqrt(-jnp.expm1(2.0 * log_a)) * (i * xf)

    def step(h, au):
        a_t, u_t = au
        h = a_t * h + u_t
        return h, h

    h_last, hs = lax.scan(step, h0.astype(jnp.float32), (a.swapaxes(0, 1), u.swapaxes(0, 1)))
    return hs.swapaxes(0, 1).astype(x.dtype), h_last


def layer(x, pos, conv_buf, h0, attend, norm_mix, w_in, q_norm, kv_norm, w_uq, w_uk, w_uv,
          conv_w, conv_b, w_ra, b_ra, w_ri, b_ri, lru_lambda, w_o, norm_ffn, w_gate, w_up, w_down):
    B, T = x.shape[0], x.shape[1]
    h = rmsnorm(x, norm_mix)
    z = h @ w_in
    q_c, kv_c, k_pe, xr, gr = jnp.split(z, SPLITS, axis=-1)
    q = (rmsnorm(q_c, q_norm) @ w_uq).reshape(B, T, N_HEADS, QK_NOPE + QK_ROPE)
    q_pe = rope(q[..., QK_NOPE:], pos)
    q_lat = jnp.einsum('bthd,lhd->bthl', q[..., :QK_NOPE], w_uk)
    ckv = rmsnorm(kv_c, kv_norm)
    kpe = rope(k_pe, pos)
    o_lat = attend(q_lat, q_pe, ckv, kpe)
    o_attn = jnp.einsum('bthl,lhd->bthd', o_lat, w_uv).reshape(B, T, D_ATTN)
    xc, conv_new = causal_conv(xr, conv_buf, conv_w, conv_b)
    hs, h_last = rglru(xc, h0, w_ra, b_ra, w_ri, b_ri, lru_lambda)
    o_rnn = hs * jax.nn.gelu(gr)
    x = x + jnp.concatenate([o_attn, o_rnn], axis=-1) @ w_o
    f = rmsnorm(x, norm_ffn)
    x = x + (jax.nn.silu(f @ w_gate) * (f @ w_up)) @ w_down
    return x, ckv, kpe, h_last, conv_new


def setup_inputs(seed: int = 0) -> dict:
    key = jax.random.key(seed)
    ks = jax.random.split(key, 32)
    n_pages = PAST_LEN // PAGE_SIZE
    n_used = DEC_BATCH * n_pages
    n_phys = n_used + n_used // 4
    f32 = jnp.float32

    def nrm(k, shape, scale):
        return jax.random.normal(k, shape, f32) * scale

    def per_layer_normal(k, shape):
        return jax.vmap(lambda kk: jax.random.normal(kk, shape, f32))(jax.random.split(k, DEPTH))

    page_table = jax.random.permutation(ks[0], n_phys)[:n_used].reshape(DEC_BATCH, n_pages).astype(jnp.int32)
    u = jax.random.uniform(ks[1], (DEPTH, D_RNN), f32, 0.9, 0.999)
    a = u ** (1.0 / LRU_C)
    lru_lambda = jnp.log(a) - jnp.log1p(-a)
    return {
        'x_prompt': nrm(ks[2], (BATCH, SEQ, D_MODEL), 1.0),
        'x_sample': nrm(ks[3], (DEC_BATCH, DEC_SEQ, D_MODEL), 1.0),
        'cache_ckv': per_layer_normal(ks[4], (n_phys, PAGE_SIZE, KV_RANK)),
        'cache_kpe': per_layer_normal(ks[5], (n_phys, PAGE_SIZE, QK_ROPE)),
        'state_h': nrm(ks[6], (DEPTH, DEC_BATCH, D_RNN), 0.5),
        'state_conv': nrm(ks[7], (DEPTH, DEC_BATCH, CONV_W - 1, D_RNN), 1.0),
        'page_table': page_table,
        'norm_mix': 1.0 + nrm(ks[8], (DEPTH, D_MODEL), 0.02),
        'w_in': nrm(ks[9], (DEPTH, D_MODEL, D_IN), D_MODEL ** -0.5),
        'q_norm': 1.0 + nrm(ks[10], (DEPTH, Q_RANK), 0.02),
        'kv_norm': 1.0 + nrm(ks[11], (DEPTH, KV_RANK), 0.02),
        'w_uq': nrm(ks[12], (DEPTH, Q_RANK, N_HEADS * (QK_NOPE + QK_ROPE)), Q_RANK ** -0.5),
        'w_uk': nrm(ks[13], (DEPTH, KV_RANK, N_HEADS, QK_NOPE), QK_NOPE ** -0.5),
        'w_uv': nrm(ks[14], (DEPTH, KV_RANK, N_HEADS, V_HEAD), KV_RANK ** -0.5),
        'conv_w': nrm(ks[15], (DEPTH, CONV_W, D_RNN), CONV_W ** -0.5),
        'conv_b': nrm(ks[16], (DEPTH, D_RNN), 0.01),
        'w_ra': nrm(ks[17], (DEPTH, N_RNN_BLOCKS, RNN_BLOCK, RNN_BLOCK), RNN_BLOCK ** -0.5),
        'b_ra': nrm(ks[18], (DEPTH, D_RNN), 0.01),
        'w_ri': nrm(ks[19], (DEPTH, N_RNN_BLOCKS, RNN_BLOCK, RNN_BLOCK), RNN_BLOCK ** -0.5),
        'b_ri': nrm(ks[20], (DEPTH, D_RNN), 0.01),
        'lru_lambda': lru_lambda,
        'w_o': nrm(ks[21], (DEPTH, MIX_WIDTH, D_MODEL), MIX_WIDTH ** -0.5),
        'norm_ffn': 1.0 + nrm(ks[22], (DEPTH, D_MODEL), 0.02),
        'w_gate': nrm(ks[23], (DEPTH, D_MODEL, D_FF), D_MODEL ** -0.5),
        'w_up': nrm(ks[24], (DEPTH, D_MODEL, D_FF), D_MODEL ** -0.5),
        'w_down': nrm(ks[25], (DEPTH, D_FF, D_MODEL), D_FF ** -0.5),
        'norm_final': 1.0 + nrm(ks[26], (D_MODEL,), 0.02),
    }


def reference(x_prompt, x_sample, cache_ckv, cache_kpe, state_h, state_conv, page_table,
              norm_mix, w_in, q_norm, kv_norm, w_uq, w_uk, w_uv, conv_w, conv_b,
              w_ra, b_ra, w_ri, b_ri, lru_lambda, w_o, norm_ffn, w_gate, w_up, w_down, norm_final):
    Bp, Tp = x_prompt.shape[0], x_prompt.shape[1]
    pos_p = jnp.arange(Tp)
    pos_s = PAST_LEN + jnp.arange(x_sample.shape[1])
    conv0 = jnp.zeros((Bp, CONV_W - 1, D_RNN), x_prompt.dtype)
    h0 = jnp.zeros((Bp, D_RNN), jnp.float32)
    xp, xs = x_prompt, x_sample
    p_ckv, p_kpe, p_h, p_conv = [], [], [], []
    s_ckv, s_kpe, s_h, s_conv = [], [], [], []
    for l in range(DEPTH):
        params = (norm_mix[l], w_in[l], q_norm[l], kv_norm[l], w_uq[l], w_uk[l], w_uv[l],
                  conv_w[l], conv_b[l], w_ra[l], b_ra[l], w_ri[l], b_ri[l], lru_lambda[l],
                  w_o[l], norm_ffn[l], w_gate[l], w_up[l], w_down[l])
        xp, ckv, kpe, hl, cb = layer(xp, pos_p, conv0, h0, prompt_attend, *params)
        p_ckv.append(ckv.astype(cache_ckv.dtype)); p_kpe.append(kpe.astype(cache_kpe.dtype))
        p_h.append(hl.astype(state_h.dtype)); p_conv.append(cb.astype(state_conv.dtype))
        attend_s = make_sample_attend(l, cache_ckv, cache_kpe, page_table)
        xs, ckv, kpe, hl, cb = layer(xs, pos_s, state_conv[l], state_h[l], attend_s, *params)
        s_ckv.append(ckv.astype(cache_ckv.dtype)); s_kpe.append(kpe.astype(cache_kpe.dtype))
        s_h.append(hl.astype(state_h.dtype)); s_conv.append(cb.astype(state_conv.dtype))
    y_prompt = rmsnorm(xp, norm_final)
    y_sample = rmsnorm(xs, norm_final)
    return (y_prompt, y_sample,
            jnp.stack(p_ckv), jnp.stack(p_kpe), jnp.stack(p_h), jnp.stack(p_conv),
            jnp.stack(s_ckv), jnp.stack(s_kpe), jnp.stack(s_h), jnp.stack(s_conv))
```

```python
import functools

import jax
import jax.numpy as jnp
from jax import lax
from jax.experimental import pallas as pl
from jax.experimental.pallas import tpu as pltpu

EPS = 1e-6
ROPE_THETA = 10000.0
LRU_C = 8.0

LANES = 128
ROPE_GROUP = 128
VMEM_LIMIT = 56 * 1024 * 1024

F32 = jnp.float32
BF16 = jnp.bfloat16


def _tile(n, pref, mult):
    best = None
    for t in range(mult, min(n, pref) + 1, mult):
        if n % t == 0:
            best = t
    return best if best is not None else n


def _params(sem):
    return pltpu.CompilerParams(dimension_semantics=sem, vmem_limit_bytes=VMEM_LIMIT)


def _dot(a, b):
    return jnp.dot(a, b, preferred_element_type=F32)


def _dot_nt(a, b):
    return lax.dot_general(a, b, (((1,), (1,)), ((), ())), preferred_element_type=F32)


def _rmsnorm_kernel(x_ref, g_ref, o_ref):
    x = x_ref[...]
    y = x * lax.rsqrt(jnp.mean(x * x, axis=-1, keepdims=True) + EPS)
    o_ref[...] = (y * g_ref[...]).astype(o_ref.dtype)


def rmsnorm(x, g, out_dtype, row0=0, rows=None):
    m, d = x.shape
    rows = m if rows is None else rows
    tm = _tile(rows, 512, 16)
    assert row0 % tm == 0
    off = row0 // tm
    return pl.pallas_call(
        _rmsnorm_kernel,
        grid=(rows // tm,),
        in_specs=[pl.BlockSpec((tm, d), lambda i: (i + off, 0)),
                  pl.BlockSpec((1, d), lambda i: (0, 0))],
        out_specs=pl.BlockSpec((tm, d), lambda i: (i, 0)),
        out_shape=jax.ShapeDtypeStruct((rows, d), out_dtype),
        compiler_params=_params(("parallel",)),
        name="rmsnorm",
    )(x, g.reshape(1, d))


def _mm_kernel(x_ref, w_ref, o_ref):
    o_ref[...] = _dot(x_ref[...], w_ref[...]).astype(o_ref.dtype)


def _mm_res_kernel(x_ref, w_ref, r_ref, o_ref):
    o_ref[...] = r_ref[...] + _dot(x_ref[...], w_ref[...])


def _swiglu_kernel(x_ref, wg_ref, wu_ref, o_ref):
    x = x_ref[...]
    g = _dot(x, wg_ref[...])
    u = _dot(x, wu_ref[...])
    o_ref[...] = (jax.nn.silu(g) * u).astype(o_ref.dtype)


def _mm_kres_kernel(x_ref, w_ref, r_ref, o_ref, acc_ref):
    k = pl.program_id(2)

    @pl.when(k == 0)
    def _():
        acc_ref[...] = jnp.zeros_like(acc_ref)

    acc_ref[...] += _dot(x_ref[...], w_ref[...])

    @pl.when(k == pl.num_programs(2) - 1)
    def _():
        o_ref[...] = r_ref[...] + acc_ref[...]


def matmul(x, w, out_dtype, tm_pref=1088, tn_pref=1024):
    m, k = x.shape
    n = w.shape[1]
    tm, tn = _tile(m, tm_pref, 16), _tile(n, tn_pref, LANES)
    return pl.pallas_call(
        _mm_kernel,
        grid=(m // tm, n // tn),
        in_specs=[pl.BlockSpec((tm, k), lambda i, j: (i, 0)),
                  pl.BlockSpec((k, tn), lambda i, j: (0, j))],
        out_specs=pl.BlockSpec((tm, tn), lambda i, j: (i, j)),
        out_shape=jax.ShapeDtypeStruct((m, n), out_dtype),
        compiler_params=_params(("parallel", "parallel")),
        name="matmul",
    )(x, w)


def matmul_res(x, w, res, tm_pref=1088, tn_pref=512):
    m, k = x.shape
    n = w.shape[1]
    tm, tn = _tile(m, tm_pref, 16), _tile(n, tn_pref, LANES)
    return pl.pallas_call(
        _mm_res_kernel,
        grid=(m // tm, n // tn),
        in_specs=[pl.BlockSpec((tm, k), lambda i, j: (i, 0)),
                  pl.BlockSpec((k, tn), lambda i, j: (0, j)),
                  pl.BlockSpec((tm, tn), lambda i, j: (i, j))],
        out_specs=pl.BlockSpec((tm, tn), lambda i, j: (i, j)),
        out_shape=jax.ShapeDtypeStruct((m, n), F32),
        compiler_params=_params(("parallel", "parallel")),
        name="matmul_res",
    )(x, w, res)


def swiglu(x, wg, wu, tm_pref=1088, tn_pref=512):
    m, k = x.shape
    n = wg.shape[1]
    tm, tn = _tile(m, tm_pref, 16), _tile(n, tn_pref, LANES)
    return pl.pallas_call(
        _swiglu_kernel,
        grid=(m // tm, n // tn),
        in_specs=[pl.BlockSpec((tm, k), lambda i, j: (i, 0)),
                  pl.BlockSpec((k, tn), lambda i, j: (0, j)),
                  pl.BlockSpec((k, tn), lambda i, j: (0, j))],
        out_specs=pl.BlockSpec((tm, tn), lambda i, j: (i, j)),
        out_shape=jax.ShapeDtypeStruct((m, n), BF16),
        compiler_params=_params(("parallel", "parallel")),
        name="swiglu",
    )(x, wg, wu)


def matmul_kres(x, w, res, tm_pref=1088, tn_pref=1024, tk_pref=2816):
    m, k = x.shape
    n = w.shape[1]
    tm, tn, tk = _tile(m, tm_pref, 16), _tile(n, tn_pref, LANES), _tile(k, tk_pref, LANES)
    return pl.pallas_call(
        _mm_kres_kernel,
        grid=(m // tm, n // tn, k // tk),
        in_specs=[pl.BlockSpec((tm, tk), lambda i, j, kk: (i, kk)),
                  pl.BlockSpec((tk, tn), lambda i, j, kk: (kk, j)),
                  pl.BlockSpec((tm, tn), lambda i, j, kk: (i, j))],
        out_specs=pl.BlockSpec((tm, tn), lambda i, j, kk: (i, j)),
        out_shape=jax.ShapeDtypeStruct((m, n), F32),
        scratch_shapes=[pltpu.VMEM((tm, tn), F32)],
        compiler_params=_params(("parallel", "parallel", "arbitrary")),
        name="matmul_kres",
    )(x, w, res)


def _rope_group(x, cs, sn):
    return x * cs + pltpu.roll(x, ROPE_GROUP // 4, 1) * sn


def _mla_prep_kernel(qc_ref, kvc_ref, kpe_ref, cs_ref, sn_ref, qg_ref, kvg_ref,
                     wqn_ref, wqp_ref, wuk_ref,
                     qcat_ref, ckv_ref, kpeo_ref, kcat_ref, *, n_heads, nope, kv_rank, rope):
    tq = qcat_ref.shape[2]
    n_sub = qcat_ref.shape[0]
    cs = cs_ref[...]
    sn = sn_ref[...]

    qc = qc_ref[...]
    qn = (qc * lax.rsqrt(jnp.mean(qc * qc, axis=-1, keepdims=True) + EPS) * qg_ref[...]).astype(BF16)
    q_nope = _dot(qn, wqn_ref[...])
    q_pe = _dot(qn, wqp_ref[...])
    for h in range(n_heads):
        ql = _dot(q_nope[:, h * nope:(h + 1) * nope].astype(BF16), wuk_ref[h]).astype(BF16)
        rp = _rope_group(q_pe[:, h * ROPE_GROUP:(h + 1) * ROPE_GROUP], cs, sn).astype(BF16)
        for t in range(n_sub):
            qcat_ref[t, h, :, 0:kv_rank] = ql[t * tq:(t + 1) * tq]
            qcat_ref[t, h, :, kv_rank:kv_rank + ROPE_GROUP] = rp[t * tq:(t + 1) * tq]

    kvc = kvc_ref[...]
    ckv = kvc * lax.rsqrt(jnp.mean(kvc * kvc, axis=-1, keepdims=True) + EPS) * kvg_ref[...]
    kr = _rope_group(kpe_ref[...], cs, sn)
    ckv_ref[...] = ckv
    kpeo_ref[...] = kr[:, 0:rope]
    kcat_ref[:, 0:kv_rank] = ckv.astype(BF16)
    kcat_ref[:, kv_rank:kv_rank + ROPE_GROUP] = kr.astype(BF16)


def mla_prep(z, lay, cs, sn, q_norm, kv_norm, wq_nope, wq_pe, wuk_t, tq):
    nt = z.shape[0]
    qr, kv, rope = lay["q_rank"], lay["kv_rank"], lay["rope"]
    n_heads, nope = wuk_t.shape[0], wuk_t.shape[1]
    dq = kv + ROPE_GROUP
    tm = 2 * tq if nt % (2 * tq) == 0 else tq
    n_sub = tm // tq
    assert nt % tm == 0 and qr % kv == 0 and lay["off_kpe"] % ROPE_GROUP == 0
    kern = functools.partial(_mla_prep_kernel, n_heads=n_heads, nope=nope, kv_rank=kv, rope=rope)
    const2 = lambda i: (0, 0)
    return pl.pallas_call(
        kern,
        grid=(nt // tm,),
        in_specs=[pl.BlockSpec((tm, qr), lambda i: (i, 0)),
                  pl.BlockSpec((tm, kv), lambda i: (i, qr // kv)),
                  pl.BlockSpec((tm, ROPE_GROUP), lambda i: (i, lay["off_kpe"] // ROPE_GROUP)),
                  pl.BlockSpec((tm, ROPE_GROUP), lambda i: (i, 0)),
                  pl.BlockSpec((tm, ROPE_GROUP), lambda i: (i, 0)),
                  pl.BlockSpec((1, qr), const2),
                  pl.BlockSpec((1, kv), const2),
                  pl.BlockSpec(wq_nope.shape, const2),
                  pl.BlockSpec(wq_pe.shape, const2),
                  pl.BlockSpec(wuk_t.shape, lambda i: (0, 0, 0))],
        out_specs=[pl.BlockSpec((n_sub, n_heads, tq, dq), lambda i: (i, 0, 0, 0)),
                   pl.BlockSpec((tm, kv), lambda i: (i, 0)),
                   pl.BlockSpec((tm, rope), lambda i: (i, 0)),
                   pl.BlockSpec((tm, dq), lambda i: (i, 0))],
        out_shape=[jax.ShapeDtypeStruct((nt // tq, n_heads, tq, dq), BF16),
                   jax.ShapeDtypeStruct((nt, kv), F32),
                   jax.ShapeDtypeStruct((nt, rope), F32),
                   jax.ShapeDtypeStruct((nt, dq), BF16)],
        compiler_params=_params(("parallel",)),
        name="mla_prep",
    )(z, z, z, cs, sn, q_norm.reshape(1, qr), kv_norm.reshape(1, kv), wq_nope, wq_pe, wuk_t)


def _prompt_attn_kernel(q_ref, k_ref, wuv_ref, o_ref, m_ref, l_ref, acc_ref, *, tq, tk, kv_rank, scale):
    i = pl.program_id(1)
    j = pl.program_id(2)
    n_heads = q_ref.shape[1]
    rows = n_heads * tq

    @pl.when(j == 0)
    def _():
        m_ref[...] = jnp.full_like(m_ref, -jnp.inf)
        l_ref[...] = jnp.zeros_like(l_ref)
        acc_ref[...] = jnp.zeros_like(acc_ref)

    last = ((i + 1) * tq - 1) // tk

    @pl.when(j <= last)
    def _():
        q = q_ref[0].reshape(rows, q_ref.shape[3])
        k = k_ref[...]
        s = _dot_nt(q, k) * scale
        q_pos = i * tq + (lax.broadcasted_iota(jnp.int32, (rows, 1), 0) & (tq - 1))
        k_pos = j * tk + lax.broadcasted_iota(jnp.int32, (1, tk), 1)
        s = jnp.where(k_pos <= q_pos, s, -jnp.inf)
        m_prev = m_ref[...]
        m_new = jnp.maximum(m_prev, jnp.max(s, axis=-1, keepdims=True))
        alpha = jnp.exp(m_prev - m_new)
        p = jnp.exp(s - m_new)
        l_ref[...] = alpha * l_ref[...] + jnp.sum(p, axis=-1, keepdims=True)
        acc_ref[...] = alpha * acc_ref[...] + _dot(p.astype(BF16), k[:, 0:kv_rank])
        m_ref[...] = m_new

    @pl.when(j == pl.num_programs(2) - 1)
    def _():
        vh = wuv_ref.shape[2]
        for h in range(n_heads):
            sl = slice(h * tq, (h + 1) * tq)
            o_lat = (acc_ref[sl, :] / l_ref[sl, :]).astype(BF16)
            o_ref[:, h * vh:(h + 1) * vh] = _dot(o_lat, wuv_ref[h]).astype(o_ref.dtype)


def prompt_attention(qcat, kcat, wuv, batch, seq, kv_rank, scale):
    n_heads, tq, dq = qcat.shape[1], qcat.shape[2], qcat.shape[3]
    assert tq & (tq - 1) == 0 and seq % tq == 0
    tk = _tile(seq, 512, tq)
    nq, nk = seq // tq, seq // tk
    vh = wuv.shape[2]
    rows = n_heads * tq
    kern = functools.partial(_prompt_attn_kernel, tq=tq, tk=tk, kv_rank=kv_rank, scale=scale)

    def k_map(b, i, j):
        return (b * nk + jnp.minimum(j, ((i + 1) * tq - 1) // tk), 0)

    return pl.pallas_call(
        kern,
        grid=(batch, nq, nk),
        in_specs=[pl.BlockSpec((1, n_heads, tq, dq), lambda b, i, j: (b * nq + i, 0, 0, 0)),
                  pl.BlockSpec((tk, dq), k_map),
                  pl.BlockSpec(wuv.shape, lambda b, i, j: (0, 0, 0))],
        out_specs=pl.BlockSpec((tq, n_heads * vh), lambda b, i, j: (b * nq + i, 0)),
        out_shape=jax.ShapeDtypeStruct((batch * seq, n_heads * vh), BF16),
        scratch_shapes=[pltpu.VMEM((rows, 1), F32), pltpu.VMEM((rows, 1), F32),
                        pltpu.VMEM((rows, kv_rank), F32)],
        compiler_params=_params(("parallel", "parallel", "arbitrary")),
        name="prompt_attention",
    )(qcat, kcat, wuv)


def _sample_attn_kernel(pt_ref, q_ref, kn_ref, *rest, n_pages_step, page, kv_rank, rope, dec_seq, scale):
    del pt_ref
    ckv_refs = rest[:n_pages_step]
    kpe_refs = rest[n_pages_step:2 * n_pages_step]
    o_ref, kc_buf, kp_buf, m_ref, l_ref, acc_ref = rest[2 * n_pages_step:]
    c = pl.program_id(1)

    @pl.when(c == 0)
    def _():
        m_ref[...] = jnp.full_like(m_ref, -jnp.inf)
        l_ref[...] = jnp.zeros_like(l_ref)
        acc_ref[...] = jnp.zeros_like(acc_ref)
        kp_buf[:, rope:ROPE_GROUP] = jnp.zeros((kp_buf.shape[0], ROPE_GROUP - rope), BF16)

    q = q_ref[0]
    q_lat = q[:, 0:kv_rank]
    q_pe = q[:, kv_rank:kv_rank + ROPE_GROUP]

    def update(s, v):
        m_prev = m_ref[...]
        m_new = jnp.maximum(m_prev, jnp.max(s, axis=-1, keepdims=True))
        alpha = jnp.exp(m_prev - m_new)
        p = jnp.exp(s - m_new)
        l_ref[...] = alpha * l_ref[...] + jnp.sum(p, axis=-1, keepdims=True)
        acc_ref[...] = alpha * acc_ref[...] + _dot(p.astype(BF16), v)
        m_ref[...] = m_new

    for p_i in range(n_pages_step):
        kc_buf[p_i * page:(p_i + 1) * page, :] = ckv_refs[p_i][0, 0].astype(BF16)
        kp_buf[p_i * page:(p_i + 1) * page, 0:rope] = kpe_refs[p_i][0, 0].astype(BF16)
    kc = kc_buf[...]
    s = (_dot_nt(q_lat, kc) + _dot_nt(q_pe, kp_buf[...])) * scale
    update(s, kc)

    @pl.when(c == pl.num_programs(1) - 1)
    def _():
        kn = kn_ref[0]
        kn_lat = kn[:, 0:kv_rank]
        s_new = (_dot_nt(q_lat, kn_lat) + _dot_nt(q_pe, kn[:, kv_rank:kv_rank + ROPE_GROUP])) * scale
        rows = q.shape[0]
        q_t = lax.broadcasted_iota(jnp.int32, (rows, 1), 0) % dec_seq
        k_t = lax.broadcasted_iota(jnp.int32, (1, kn.shape[0]), 1)
        update(jnp.where(k_t <= q_t, s_new, -jnp.inf), kn_lat)
        o_ref[0] = (acc_ref[...] / l_ref[...]).astype(o_ref.dtype)


def sample_attention(q_s, k_new, cache_ckv, cache_kpe, page_table, layer, scale, pages_per_step=16):
    n_seq, rows, dq = q_s.shape
    dec_seq = k_new.shape[1]
    k_new = jnp.pad(k_new, ((0, 0), (0, LANES - dec_seq), (0, 0)))
    page, kv_rank = cache_ckv.shape[2], cache_ckv.shape[3]
    rope = cache_kpe.shape[3]
    n_pages = page_table.shape[1]
    pps = _tile(n_pages, pages_per_step, 1)
    kern = functools.partial(_sample_attn_kernel, n_pages_step=pps, page=page, kv_rank=kv_rank,
                             rope=rope, dec_seq=dec_seq, scale=scale)

    def page_map(p_i):
        return lambda b, c, pt: (layer, pt[b, c * pps + p_i], 0, 0)

    in_specs = [pl.BlockSpec((1, rows, dq), lambda b, c, pt: (b, 0, 0)),
                pl.BlockSpec((1, LANES, dq), lambda b, c, pt: (b, 0, 0))]
    in_specs += [pl.BlockSpec((1, 1, page, kv_rank), page_map(p_i)) for p_i in range(pps)]
    in_specs += [pl.BlockSpec((1, 1, page, rope), page_map(p_i)) for p_i in range(pps)]
    grid_spec = pltpu.PrefetchScalarGridSpec(
        num_scalar_prefetch=1,
        grid=(n_seq, n_pages // pps),
        in_specs=in_specs,
        out_specs=pl.BlockSpec((1, rows, kv_rank), lambda b, c, pt: (b, 0, 0)),
        scratch_shapes=[pltpu.VMEM((pps * page, kv_rank), BF16), pltpu.VMEM((pps * page, ROPE_GROUP), BF16),
                        pltpu.VMEM((rows, 1), F32), pltpu.VMEM((rows, 1), F32),
                        pltpu.VMEM((rows, kv_rank), F32)])
    return pl.pallas_call(
        kern,
        grid_spec=grid_spec,
        out_shape=jax.ShapeDtypeStruct((n_seq, rows, kv_rank), BF16),
        compiler_params=_params(("parallel", "arbitrary")),
        name="sample_attention",
    )(page_table, q_s, k_new, *([cache_ckv] * pps), *([cache_kpe] * pps))


def _uv_proj_kernel(o_ref, w_ref, out_ref):
    out_ref[...] = _dot(o_ref[0], w_ref[0]).astype(out_ref.dtype)


def uv_proj(o_lat_hm, wuv):
    n_heads, n, kv_rank = o_lat_hm.shape
    vh = wuv.shape[2]
    return pl.pallas_call(
        _uv_proj_kernel,
        grid=(n_heads,),
        in_specs=[pl.BlockSpec((1, n, kv_rank), lambda h: (h, 0, 0)),
                  pl.BlockSpec((1, kv_rank, vh), lambda h: (h, 0, 0))],
        out_specs=pl.BlockSpec((n, vh), lambda h: (0, h)),
        out_shape=jax.ShapeDtypeStruct((n, n_heads * vh), BF16),
        compiler_params=_params(("parallel",)),
        name="uv_proj",
    )(o_lat_hm, wuv)


def _lru_gates(xc, wra, bra, wri, bri, lam):
    xb = xc.astype(BF16)
    r = jax.nn.sigmoid(_dot(xb, wra) + bra)
    i = jax.nn.sigmoid(_dot(xb, wri) + bri)
    log_a = -LRU_C * r * jax.nn.softplus(-lam)
    a = jnp.exp(log_a)
    one_minus_a2 = -jnp.tanh(log_a) * (a * a + 1.0)
    u = jnp.sqrt(one_minus_a2) * (i * xc)
    return a, u


def _rnn_gates_prompt_kernel(x_ref, cw_ref, cb_ref, wra_ref, bra_ref, wri_ref, bri_ref, lam_ref,
                             a_ref, u_ref, *, conv_w):
    x = x_ref[...]
    t_idx = lax.broadcasted_iota(jnp.int32, (x.shape[0], 1), 0)
    cw = cw_ref[...]
    acc = None
    for k in range(conv_w):
        d = conv_w - 1 - k
        xs = x if d == 0 else jnp.where(t_idx >= d, pltpu.roll(x, d, 0), 0.0)
        term = xs * cw[k:k + 1, :]
        acc = term if acc is None else acc + term
    xc = cb_ref[...] + acc
    a, u = _lru_gates(xc, wra_ref[0], bra_ref[...], wri_ref[0], bri_ref[...], lam_ref[...])
    a_ref[...] = a
    u_ref[...] = u


def rnn_gates_prompt(z, lay, batch, seq, conv_w, conv_b, wra, bra, wri, bri, lam):
    d_rnn = lay["d_rnn"]
    nb, cb = wra.shape[0], wra.shape[1]
    assert cb == LANES
    x_off = lay["off_xr"] // cb
    kern = functools.partial(_rnn_gates_prompt_kernel, conv_w=conv_w.shape[0])
    vec = lambda b, n: (0, n)
    blk = lambda b, n: (n, 0, 0)
    a, u = pl.pallas_call(
        kern,
        grid=(batch, nb),
        in_specs=[pl.BlockSpec((seq, cb), lambda b, n: (b, x_off + n)),
                  pl.BlockSpec((conv_w.shape[0], cb), vec),
                  pl.BlockSpec((1, cb), vec),
                  pl.BlockSpec((1, cb, cb), blk), pl.BlockSpec((1, cb), vec),
                  pl.BlockSpec((1, cb, cb), blk), pl.BlockSpec((1, cb), vec),
                  pl.BlockSpec((1, cb), vec)],
        out_specs=[pl.BlockSpec((seq, cb), lambda b, n: (b, n))] * 2,
        out_shape=[jax.ShapeDtypeStruct((batch * seq, d_rnn), F32)] * 2,
        compiler_params=_params(("parallel", "parallel")),
        name="rnn_gates_prompt",
    )(z, conv_w, conv_b.reshape(1, d_rnn), wra, bra.reshape(1, d_rnn), wri, bri.reshape(1, d_rnn),
      lam.reshape(1, d_rnn))
    return a, u


def _rnn_scan_prompt_kernel(a_ref, u_ref, g_ref, o_ref, hl_ref, h_ref, hs_ref):
    tt = pl.program_id(1)

    @pl.when(tt == 0)
    def _():
        h_ref[...] = jnp.zeros_like(h_ref)

    def step(t, h):
        h = a_ref[t] * h + u_ref[t]
        hs_ref[t] = h
        return h

    h = lax.fori_loop(0, a_ref.shape[0], step, h_ref[...], unroll=8)
    h_ref[...] = h
    o_ref[...] = (hs_ref[...] * jax.nn.gelu(g_ref[...])).astype(o_ref.dtype)

    @pl.when(tt == pl.num_programs(1) - 1)
    def _():
        hl_ref[0] = h


def rnn_scan_prompt(a, u, z, lay, batch, seq):
    d_rnn = lay["d_rnn"]
    nb = d_rnn // LANES
    assert lay["off_gr"] % d_rnn == 0 and z.shape[1] % LANES == 0
    tc = _tile(seq, 512, 8)
    nt = seq // tc
    a4 = a.reshape(batch * seq, nb, LANES)
    u4 = u.reshape(batch * seq, nb, LANES)
    z4 = z.reshape(z.shape[0], z.shape[1] // LANES, LANES)
    g_blk = lay["off_gr"] // d_rnn
    row = lambda b, t: (b * nt + t, 0, 0)
    o, hl = pl.pallas_call(
        _rnn_scan_prompt_kernel,
        grid=(batch, nt),
        in_specs=[pl.BlockSpec((tc, nb, LANES), row),
                  pl.BlockSpec((tc, nb, LANES), row),
                  pl.BlockSpec((tc, nb, LANES), lambda b, t: (b * nt + t, g_blk, 0))],
        out_specs=[pl.BlockSpec((tc, nb, LANES), row),
                   pl.BlockSpec((1, nb, LANES), lambda b, t: (b, 0, 0))],
        out_shape=[jax.ShapeDtypeStruct((batch * seq, nb, LANES), BF16),
                   jax.ShapeDtypeStruct((batch, nb, LANES), F32)],
        scratch_shapes=[pltpu.VMEM((nb, LANES), F32), pltpu.VMEM((tc, nb, LANES), F32)],
        compiler_params=_params(("parallel", "arbitrary")),
        name="rnn_scan_prompt",
    )(a4, u4, z4)
    return o.reshape(batch * seq, d_rnn), hl.reshape(batch, d_rnn)


def _rnn_sample_kernel(x_ref, st_ref, g_ref, h0_ref, cw_ref, cb_ref, wra_ref, bra_ref, wri_ref, bri_ref,
                       lam_ref, o_ref, hl_ref, *, conv_w):
    dec_seq, n_seq = x_ref.shape[0], x_ref.shape[1]
    cw = cw_ref[...]
    xp = [st_ref[k] for k in range(conv_w - 1)] + [x_ref[t] for t in range(dec_seq)]
    xc = []
    for t in range(dec_seq):
        acc = None
        for k in range(conv_w):
            term = xp[t + k] * cw[k:k + 1, :]
            acc = term if acc is None else acc + term
        xc.append(cb_ref[...] + acc)
    a, u = _lru_gates(jnp.concatenate(xc, axis=0), wra_ref[0], bra_ref[...], wri_ref[0], bri_ref[...],
                      lam_ref[...])
    h = h0_ref[...]
    for t in range(dec_seq):
        h = a[t * n_seq:(t + 1) * n_seq] * h + u[t * n_seq:(t + 1) * n_seq]
        o_ref[t] = (h * jax.nn.gelu(g_ref[t])).astype(o_ref.dtype)
    hl_ref[...] = h


def rnn_sample(x_t, st_t, g_t, h0, conv_w, conv_b, wra, bra, wri, bri, lam):
    dec_seq, n_seq, d_rnn = x_t.shape
    nb, cb = wra.shape[0], wra.shape[1]
    cwid = conv_w.shape[0]
    kern = functools.partial(_rnn_sample_kernel, conv_w=cwid)
    vec = lambda n: (0, n)
    blk = lambda n: (n, 0, 0)
    t3 = lambda n: (0, 0, n)
    return pl.pallas_call(
        kern,
        grid=(nb,),
        in_specs=[pl.BlockSpec((dec_seq, n_seq, cb), t3),
                  pl.BlockSpec((cwid - 1, n_seq, cb), t3),
                  pl.BlockSpec((dec_seq, n_seq, cb), t3),
                  pl.BlockSpec((n_seq, cb), vec),
                  pl.BlockSpec((cwid, cb), vec), pl.BlockSpec((1, cb), vec),
                  pl.BlockSpec((1, cb, cb), blk), pl.BlockSpec((1, cb), vec),
                  pl.BlockSpec((1, cb, cb), blk), pl.BlockSpec((1, cb), vec),
                  pl.BlockSpec((1, cb), vec)],
        out_specs=[pl.BlockSpec((dec_seq, n_seq, cb), t3),
                   pl.BlockSpec((n_seq, cb), vec)],
        out_shape=[jax.ShapeDtypeStruct((dec_seq, n_seq, d_rnn), BF16),
                   jax.ShapeDtypeStruct((n_seq, d_rnn), F32)],
        compiler_params=_params(("parallel",)),
        name="rnn_sample",
    )(x_t, st_t, g_t, h0, conv_w, conv_b.reshape(1, d_rnn), wra, bra.reshape(1, d_rnn),
      wri, bri.reshape(1, d_rnn), lam.reshape(1, d_rnn))


def _z_layout(q_rank, kv_rank, rope, d_rnn):
    off_kpe = q_rank + kv_rank
    off_xr = -(-(off_kpe + ROPE_GROUP) // d_rnn) * d_rnn
    off_gr = off_xr + d_rnn
    return dict(q_rank=q_rank, kv_rank=kv_rank, rope=rope, d_rnn=d_rnn, off_kpe=off_kpe,
                off_xr=off_xr, off_gr=off_gr, width=off_gr + d_rnn)


def _dup_rope_cols(w, rope):
    return jnp.concatenate([w, w], axis=-1)


def _prep_layer_weights(lay, w_in, w_uq, w_uk, w_uv, w_ra, w_ri, w_o, w_gate, w_up, w_down, d_ff_pad):
    q_rank, kv_rank, rope, d_rnn = lay["q_rank"], lay["kv_rank"], lay["rope"], lay["d_rnn"]
    d_model = w_in.shape[0]
    n_heads, nope = w_uk.shape[1], w_uk.shape[2]
    o = q_rank + kv_rank
    pad = lay["off_xr"] - (lay["off_kpe"] + ROPE_GROUP)
    w_in_p = jnp.concatenate(
        [w_in[:, :o], _dup_rope_cols(w_in[:, o:o + rope], rope), jnp.zeros((d_model, pad), w_in.dtype),
         w_in[:, o + rope:]], axis=1).astype(BF16)
    wq = w_uq.reshape(q_rank, n_heads, nope + rope)
    wq_nope = wq[:, :, :nope].reshape(q_rank, n_heads * nope).astype(BF16)
    wq_pe = _dup_rope_cols(wq[:, :, nope:], rope).reshape(q_rank, n_heads * ROPE_GROUP).astype(BF16)
    wuk_t = jnp.transpose(w_uk, (1, 2, 0)).astype(BF16)
    wuv_h = jnp.transpose(w_uv, (1, 0, 2)).astype(BF16)
    d_ff = w_gate.shape[1]
    fpad = d_ff_pad - d_ff
    wg = jnp.pad(w_gate, ((0, 0), (0, fpad))).astype(BF16)
    wu = jnp.pad(w_up, ((0, 0), (0, fpad))).astype(BF16)
    wd = jnp.pad(w_down, ((0, fpad), (0, 0))).astype(BF16)
    return dict(w_in=w_in_p, wq_nope=wq_nope, wq_pe=wq_pe, wuk_t=wuk_t, wuv=wuv_h,
                w_ra=w_ra.astype(BF16), w_ri=w_ri.astype(BF16), w_o=w_o.astype(BF16), wg=wg, wu=wu, wd=wd)


def _rope_tables(pos, rope):
    half = rope // 2
    inv = ROPE_THETA ** (-jnp.arange(half, dtype=F32) / half)
    ang = pos.astype(F32)[:, None] * inv[None, :]
    c, s = jnp.cos(ang), jnp.sin(ang)
    zero = jnp.zeros_like(c)
    return (jnp.concatenate([c, c, zero, zero], axis=1),
            jnp.concatenate([-s, s, zero, zero], axis=1))


def kernel(x_prompt, x_sample, cache_ckv, cache_kpe, state_h, state_conv, page_table, norm_mix, w_in, q_norm, kv_norm, w_uq, w_uk, w_uv, conv_w, conv_b, w_ra, b_ra, w_ri, b_ri, lru_lambda, w_o, norm_ffn, w_gate, w_up, w_down, norm_final):
    batch, seq, d_model = x_prompt.shape
    n_seq, dec_seq, _ = x_sample.shape
    depth = norm_mix.shape[0]
    q_rank, kv_rank = q_norm.shape[1], kv_norm.shape[1]
    rope = cache_kpe.shape[3]
    n_heads, nope = w_uk.shape[2], w_uk.shape[3]
    d_rnn = state_h.shape[2]
    cwid = conv_w.shape[1]
    d_ff = w_gate.shape[2]
    past_len = page_table.shape[1] * cache_ckv.shape[2]
    assert 4 * (rope // 2) == ROPE_GROUP and dec_seq >= cwid - 1 and seq >= cwid - 1
    scale = float(nope + rope) ** -0.5
    lay = _z_layout(q_rank, kv_rank, rope, d_rnn)
    d_ff_pad = -(-d_ff // 1024) * 1024 if d_ff > 1024 else d_ff
    tq = 128

    n_p, n_s = batch * seq, n_seq * dec_seq
    nt = n_p + n_s
    assert n_p % tq == 0 and n_s % tq == 0 and tq % dec_seq == 0
    pos = jnp.concatenate([jnp.tile(jnp.arange(seq), batch), jnp.tile(past_len + jnp.arange(dec_seq), n_seq)])
    cs, sn = _rope_tables(pos, rope)

    x = jnp.concatenate([x_prompt.reshape(n_p, d_model), x_sample.reshape(n_s, d_model)], axis=0)
    outs = {k: [] for k in ("p_ckv", "p_kpe", "p_h", "p_conv", "s_ckv", "s_kpe", "s_h", "s_conv")}
    for l in range(depth):
        w = _prep_layer_weights(lay, w_in[l], w_uq[l], w_uk[l], w_uv[l], w_ra[l], w_ri[l], w_o[l],
                                w_gate[l], w_up[l], w_down[l], d_ff_pad)
        h = rmsnorm(x, norm_mix[l], BF16)
        z = matmul(h, w["w_in"], F32)
        qcat, ckv, kpe, kcat = mla_prep(z, lay, cs, sn, q_norm[l], kv_norm[l],
                                        w["wq_nope"], w["wq_pe"], w["wuk_t"], tq)
        dq = kcat.shape[1]
        o_attn_p = prompt_attention(qcat, kcat, w["wuv"], batch, seq, kv_rank, scale)
        seq_per_tile = tq // dec_seq
        q_s = qcat[n_p // tq:].reshape(n_s // tq, n_heads, seq_per_tile, dec_seq, dq)
        q_s = jnp.transpose(q_s, (0, 2, 1, 3, 4)).reshape(n_seq, n_heads * dec_seq, dq)
        k_new = kcat[n_p:].reshape(n_seq, dec_seq, dq)
        o_lat_s = sample_attention(q_s, k_new, cache_ckv, cache_kpe, page_table, l, scale)
        o_lat_s = jnp.transpose(o_lat_s.reshape(n_seq, n_heads, dec_seq, kv_rank), (1, 0, 2, 3))
        o_attn_s = uv_proj(o_lat_s.reshape(n_heads, n_s, kv_rank), w["wuv"])
        a_p, u_p = rnn_gates_prompt(z, lay, batch, seq, conv_w[l], conv_b[l], w["w_ra"], b_ra[l],
                                    w["w_ri"], b_ri[l], lru_lambda[l])
        o_rnn_p, h_last_p = rnn_scan_prompt(a_p, u_p, z, lay, batch, seq)
        xr_p = z[:n_p, lay["off_xr"]:lay["off_xr"] + d_rnn].reshape(batch, seq, d_rnn)
        z_s = z[n_p:].reshape(n_seq, dec_seq, -1)
        xr_s = z_s[:, :, lay["off_xr"]:lay["off_xr"] + d_rnn]
        gr_s = z_s[:, :, lay["off_gr"]:lay["off_gr"] + d_rnn]
        o_rnn_s, h_last_s = rnn_sample(jnp.swapaxes(xr_s, 0, 1), jnp.swapaxes(state_conv[l], 0, 1),
                                       jnp.swapaxes(gr_s, 0, 1), state_h[l], conv_w[l], conv_b[l],
                                       w["w_ra"], b_ra[l], w["w_ri"], b_ri[l], lru_lambda[l])
        o_rnn_s = jnp.swapaxes(o_rnn_s, 0, 1).reshape(n_s, d_rnn)
        mix = jnp.concatenate([jnp.concatenate([o_attn_p, o_attn_s], axis=0),
                               jnp.concatenate([o_rnn_p, o_rnn_s], axis=0)], axis=1)
        x = matmul_res(mix, w["w_o"], x)
        f = rmsnorm(x, norm_ffn[l], BF16)
        x = matmul_kres(swiglu(f, w["wg"], w["wu"]), w["wd"], x)

        outs["p_ckv"].append(ckv[:n_p].reshape(batch, seq, kv_rank))
        outs["p_kpe"].append(kpe[:n_p].reshape(batch, seq, rope))
        outs["p_h"].append(h_last_p)
        outs["p_conv"].append(xr_p[:, seq - (cwid - 1):])
        outs["s_ckv"].append(ckv[n_p:].reshape(n_seq, dec_seq, kv_rank))
        outs["s_kpe"].append(kpe[n_p:].reshape(n_seq, dec_seq, rope))
        outs["s_h"].append(h_last_s)
        outs["s_conv"].append(xr_s[:, dec_seq - (cwid - 1):])

    y_prompt = rmsnorm(x, norm_final, F32, 0, n_p).reshape(batch, seq, d_model)
    y_sample = rmsnorm(x, norm_final, F32, n_p, n_s).reshape(n_seq, dec_seq, d_model)
    st = lambda k: jnp.stack(outs[k])
    return (y_prompt, y_sample, st("p_ckv"), st("p_kpe"), st("p_h"), st("p_conv"),
            st("s_ckv"), st("s_kpe"), st("s_h"), st("s_conv"))
```

```python
import functools

import jax
import jax.numpy as jnp
from jax import lax
from jax.experimental import pallas as pl
from jax.experimental.pallas import tpu as pltpu

EPS = 1e-6
ROPE_THETA = 10000.0
LRU_C = 8.0
LOG2E = 1.4426950408889634

LANES = 128
ROPE_GROUP = 128
VMEM_LIMIT = 56 * 1024 * 1024

F32 = jnp.float32
BF16 = jnp.bfloat16


def _tile(n, pref, mult):
    best = None
    for t in range(mult, min(n, pref) + 1, mult):
        if n % t == 0:
            best = t
    return best if best is not None else n


def _params(sem):
    return pltpu.CompilerParams(dimension_semantics=sem, vmem_limit_bytes=VMEM_LIMIT)


def _dot(a, b):
    return jnp.dot(a, b, preferred_element_type=F32)


def _dot_nt(a, b):
    return lax.dot_general(a, b, (((1,), (1,)), ((), ())), preferred_element_type=F32)


def _rmsnorm_kernel(x_ref, g_ref, o_ref):
    x = x_ref[...]
    y = x * lax.rsqrt(jnp.mean(x * x, axis=-1, keepdims=True) + EPS)
    o_ref[...] = (y * g_ref[...]).astype(o_ref.dtype)


def rmsnorm(x, g, out_dtype, row0=0, rows=None):
    m, d = x.shape
    rows = m if rows is None else rows
    tm = _tile(rows, 512, 16)
    assert row0 % tm == 0
    off = row0 // tm
    return pl.pallas_call(
        _rmsnorm_kernel,
        grid=(rows // tm,),
        in_specs=[pl.BlockSpec((tm, d), lambda i: (i + off, 0)),
                  pl.BlockSpec((1, d), lambda i: (0, 0))],
        out_specs=pl.BlockSpec((tm, d), lambda i: (i, 0)),
        out_shape=jax.ShapeDtypeStruct((rows, d), out_dtype),
        compiler_params=_params(("parallel",)),
        name="rmsnorm",
    )(x, g.reshape(1, d))


def _mm_kernel(x_ref, w_ref, o_ref):
    o_ref[...] = _dot(x_ref[...], w_ref[...]).astype(o_ref.dtype)


def _mm_res_kernel(x_ref, w_ref, r_ref, o_ref):
    o_ref[...] = r_ref[...] + _dot(x_ref[...], w_ref[...])


def _swiglu_kernel(x_ref, wg_ref, wu_ref, o_ref):
    x = x_ref[...]
    g = _dot(x, wg_ref[...])
    u = _dot(x, wu_ref[...])
    o_ref[...] = (jax.nn.silu(g) * u).astype(o_ref.dtype)


def _mm_kres_kernel(x_ref, w_ref, r_ref, o_ref, acc_ref):
    k = pl.program_id(2)

    @pl.when(k == 0)
    def _():
        acc_ref[...] = jnp.zeros_like(acc_ref)

    acc_ref[...] += _dot(x_ref[...], w_ref[...])

    @pl.when(k == pl.num_programs(2) - 1)
    def _():
        o_ref[...] = r_ref[...] + acc_ref[...]


def matmul(x, w, out_dtype, tm_pref=1088, tn_pref=1024):
    m, k = x.shape
    n = w.shape[1]
    tm, tn = _tile(m, tm_pref, 16), _tile(n, tn_pref, LANES)
    return pl.pallas_call(
        _mm_kernel,
        grid=(m // tm, n // tn),
        in_specs=[pl.BlockSpec((tm, k), lambda i, j: (i, 0)),
                  pl.BlockSpec((k, tn), lambda i, j: (0, j))],
        out_specs=pl.BlockSpec((tm, tn), lambda i, j: (i, j)),
        out_shape=jax.ShapeDtypeStruct((m, n), out_dtype),
        compiler_params=_params(("parallel", "parallel")),
        name="matmul",
    )(x, w)


def matmul_res(x, w, res, tm_pref=1088, tn_pref=512):
    m, k = x.shape
    n = w.shape[1]
    tm, tn = _tile(m, tm_pref, 16), _tile(n, tn_pref, LANES)
    return pl.pallas_call(
        _mm_res_kernel,
        grid=(m // tm, n // tn),
        in_specs=[pl.BlockSpec((tm, k), lambda i, j: (i, 0)),
                  pl.BlockSpec((k, tn), lambda i, j: (0, j)),
                  pl.BlockSpec((tm, tn), lambda i, j: (i, j))],
        out_specs=pl.BlockSpec((tm, tn), lambda i, j: (i, j)),
        out_shape=jax.ShapeDtypeStruct((m, n), F32),
        compiler_params=_params(("parallel", "parallel")),
        name="matmul_res",
    )(x, w, res)


def swiglu(x, wg, wu, tm_pref=1088, tn_pref=512):
    m, k = x.shape
    n = wg.shape[1]
    tm, tn = _tile(m, tm_pref, 16), _tile(n, tn_pref, LANES)
    return pl.pallas_call(
        _swiglu_kernel,
        grid=(m // tm, n // tn),
        in_specs=[pl.BlockSpec((tm, k), lambda i, j: (i, 0)),
                  pl.BlockSpec((k, tn), lambda i, j: (0, j)),
                  pl.BlockSpec((k, tn), lambda i, j: (0, j))],
        out_specs=pl.BlockSpec((tm, tn), lambda i, j: (i, j)),
        out_shape=jax.ShapeDtypeStruct((m, n), BF16),
        compiler_params=_params(("parallel", "parallel")),
        name="swiglu",
    )(x, wg, wu)


def matmul_kres(x, w, res, tm_pref=1088, tn_pref=1024, tk_pref=2816):
    m, k = x.shape
    n = w.shape[1]
    tm, tn, tk = _tile(m, tm_pref, 16), _tile(n, tn_pref, LANES), _tile(k, tk_pref, LANES)
    return pl.pallas_call(
        _mm_kres_kernel,
        grid=(m // tm, n // tn, k // tk),
        in_specs=[pl.BlockSpec((tm, tk), lambda i, j, kk: (i, kk)),
                  pl.BlockSpec((tk, tn), lambda i, j, kk: (kk, j)),
                  pl.BlockSpec((tm, tn), lambda i, j, kk: (i, j))],
        out_specs=pl.BlockSpec((tm, tn), lambda i, j, kk: (i, j)),
        out_shape=jax.ShapeDtypeStruct((m, n), F32),
        scratch_shapes=[pltpu.VMEM((tm, tn), F32)],
        compiler_params=_params(("parallel", "parallel", "arbitrary")),
        name="matmul_kres",
    )(x, w, res)


def _rope_group(x, cs, sn):
    return x * cs + pltpu.roll(x, ROPE_GROUP // 4, 1) * sn


def _mla_prep_kernel(qc_ref, kvc_ref, kpe_ref, cs_ref, sn_ref, qg_ref, kvg_ref,
                     wqn_ref, wqp_ref, wuk_ref,
                     qcat_ref, ckv_ref, kpeo_ref, kcat_ref, *, n_heads, nope, kv_rank, rope):
    tq = qcat_ref.shape[2]
    n_sub = qcat_ref.shape[0]
    cs = cs_ref[...]
    sn = sn_ref[...]

    qc = qc_ref[...]
    qn = (qc * lax.rsqrt(jnp.mean(qc * qc, axis=-1, keepdims=True) + EPS) * qg_ref[...]).astype(BF16)
    q_nope = _dot(qn, wqn_ref[...])
    q_pe = _dot(qn, wqp_ref[...])
    for h in range(n_heads):
        ql = _dot(q_nope[:, h * nope:(h + 1) * nope].astype(BF16), wuk_ref[h]).astype(BF16)
        rp = _rope_group(q_pe[:, h * ROPE_GROUP:(h + 1) * ROPE_GROUP], cs, sn).astype(BF16)
        for t in range(n_sub):
            qcat_ref[t, h, :, 0:kv_rank] = ql[t * tq:(t + 1) * tq]
            qcat_ref[t, h, :, kv_rank:kv_rank + ROPE_GROUP] = rp[t * tq:(t + 1) * tq]

    kvc = kvc_ref[...]
    ckv = kvc * lax.rsqrt(jnp.mean(kvc * kvc, axis=-1, keepdims=True) + EPS) * kvg_ref[...]
    kr = _rope_group(kpe_ref[...], cs, sn)
    ckv_ref[...] = ckv
    kpeo_ref[...] = kr[:, 0:rope]
    kcat_ref[:, 0:kv_rank] = ckv.astype(BF16)
    kcat_ref[:, kv_rank:kv_rank + ROPE_GROUP] = kr.astype(BF16)


def mla_prep(z, lay, cs, sn, q_norm, kv_norm, wq_nope, wq_pe, wuk_t, tq):
    nt = z.shape[0]
    qr, kv, rope = lay["q_rank"], lay["kv_rank"], lay["rope"]
    n_heads, nope = wuk_t.shape[0], wuk_t.shape[1]
    dq = kv + ROPE_GROUP
    tm = 2 * tq if nt % (2 * tq) == 0 else tq
    n_sub = tm // tq
    assert nt % tm == 0 and qr % kv == 0 and lay["off_kpe"] % ROPE_GROUP == 0
    kern = functools.partial(_mla_prep_kernel, n_heads=n_heads, nope=nope, kv_rank=kv, rope=rope)
    const2 = lambda i: (0, 0)
    return pl.pallas_call(
        kern,
        grid=(nt // tm,),
        in_specs=[pl.BlockSpec((tm, qr), lambda i: (i, 0)),
                  pl.BlockSpec((tm, kv), lambda i: (i, qr // kv)),
                  pl.BlockSpec((tm, ROPE_GROUP), lambda i: (i, lay["off_kpe"] // ROPE_GROUP)),
                  pl.BlockSpec((tm, ROPE_GROUP), lambda i: (i, 0)),
                  pl.BlockSpec((tm, ROPE_GROUP), lambda i: (i, 0)),
                  pl.BlockSpec((1, qr), const2),
                  pl.BlockSpec((1, kv), const2),
                  pl.BlockSpec(wq_nope.shape, const2),
                  pl.BlockSpec(wq_pe.shape, const2),
                  pl.BlockSpec(wuk_t.shape, lambda i: (0, 0, 0))],
        out_specs=[pl.BlockSpec((n_sub, n_heads, tq, dq), lambda i: (i, 0, 0, 0)),
                   pl.BlockSpec((tm, kv), lambda i: (i, 0)),
                   pl.BlockSpec((tm, rope), lambda i: (i, 0)),
                   pl.BlockSpec((tm, dq), lambda i: (i, 0))],
        out_shape=[jax.ShapeDtypeStruct((nt // tq, n_heads, tq, dq), BF16),
                   jax.ShapeDtypeStruct((nt, kv), F32),
                   jax.ShapeDtypeStruct((nt, rope), F32),
                   jax.ShapeDtypeStruct((nt, dq), BF16)],
        compiler_params=_params(("parallel",)),
        name="mla_prep",
    )(z, z, z, cs, sn, q_norm.reshape(1, qr), kv_norm.reshape(1, kv), wq_nope, wq_pe, wuk_t)


def _prompt_attn_kernel(q_ref, k_ref, wuv_ref, o_ref, m_ref, l_ref, acc_ref,
                        *, tq, tk, kv_rank, scale, heads_per_group):
    i = pl.program_id(1)
    j = pl.program_id(2)
    n_heads = q_ref.shape[1]

    @pl.when(j == 0)
    def _():
        m_ref[...] = jnp.full_like(m_ref, -jnp.inf)
        l_ref[...] = jnp.zeros_like(l_ref)
        acc_ref[...] = jnp.zeros_like(acc_ref)

    last = ((i + 1) * tq - 1) // tk
    c2 = scale * LOG2E
    g_rows = heads_per_group * tq

    def block(masked):
        k = k_ref[...]
        v = k[:, 0:kv_rank]
        for g in range(n_heads // heads_per_group):
            rs = slice(g * g_rows, (g + 1) * g_rows)
            q = q_ref[0, g * heads_per_group:(g + 1) * heads_per_group].reshape(g_rows, q_ref.shape[3])
            s = _dot_nt(q, k)
            if masked:
                q_pos = i * tq + (lax.broadcasted_iota(jnp.int32, (g_rows, 1), 0) & (tq - 1))
                k_pos = j * tk + lax.broadcasted_iota(jnp.int32, (1, tk), 1)
                s = jnp.where(k_pos <= q_pos, s, -jnp.inf)
            m_prev = m_ref[rs, :]
            m_new = jnp.maximum(m_prev, jnp.max(s, axis=-1, keepdims=True))
            alpha = jnp.exp2((m_prev - m_new) * c2)
            p = jnp.exp2((s - m_new) * c2)
            l_ref[rs, :] = alpha * l_ref[rs, :] + jnp.sum(p, axis=-1, keepdims=True)
            acc_ref[rs, :] = alpha * acc_ref[rs, :] + _dot(p.astype(BF16), v)
            m_ref[rs, :] = m_new

    pl.when(j < last)(functools.partial(block, False))
    pl.when(j == last)(functools.partial(block, True))

    @pl.when(j == pl.num_programs(2) - 1)
    def _():
        vh = wuv_ref.shape[2]
        for h in range(n_heads):
            sl = slice(h * tq, (h + 1) * tq)
            o_lat = (acc_ref[sl, :] / l_ref[sl, :]).astype(BF16)
            o_ref[:, h * vh:(h + 1) * vh] = _dot(o_lat, wuv_ref[h]).astype(o_ref.dtype)


def prompt_attention(qcat, kcat, wuv, batch, seq, kv_rank, scale):
    n_heads, tq, dq = qcat.shape[1], qcat.shape[2], qcat.shape[3]
    assert tq & (tq - 1) == 0 and seq % tq == 0
    tk = _tile(seq, 512, tq)
    nq, nk = seq // tq, seq // tk
    vh = wuv.shape[2]
    rows = n_heads * tq
    hpg = n_heads
    kern = functools.partial(_prompt_attn_kernel, tq=tq, tk=tk, kv_rank=kv_rank, scale=scale,
                             heads_per_group=hpg)

    def k_map(b, i, j):
        return (b * nk + jnp.minimum(j, ((i + 1) * tq - 1) // tk), 0)

    return pl.pallas_call(
        kern,
        grid=(batch, nq, nk),
        in_specs=[pl.BlockSpec((1, n_heads, tq, dq), lambda b, i, j: (b * nq + i, 0, 0, 0)),
                  pl.BlockSpec((tk, dq), k_map),
                  pl.BlockSpec(wuv.shape, lambda b, i, j: (0, 0, 0))],
        out_specs=pl.BlockSpec((tq, n_heads * vh), lambda b, i, j: (b * nq + i, 0)),
        out_shape=jax.ShapeDtypeStruct((batch * seq, n_heads * vh), BF16),
        scratch_shapes=[pltpu.VMEM((rows, 1), F32), pltpu.VMEM((rows, 1), F32),
                        pltpu.VMEM((rows, kv_rank), F32)],
        compiler_params=_params(("parallel", "parallel", "arbitrary")),
        name="prompt_attention",
    )(qcat, kcat, wuv)


def _sample_attn_kernel(pt_ref, q_ref, kn_ref, ckv_hbm, kpe_hbm, o_ref,
                        ckv_in, kpe_in, kc_buf, kp_buf, m_ref, l_ref, acc_ref, sem,
                        *, layer, n_pages_step, page, kv_rank, rope, dec_seq, scale):
    b = pl.program_id(0)
    c = pl.program_id(1)
    n_chunks = pl.num_programs(1)
    step = b * n_chunks + c
    slot = step % 2

    def page_copies(bb, cc, sl):
        out = []
        for p_i in range(n_pages_step):
            pg = pt_ref[bb, cc * n_pages_step + p_i]
            out.append(pltpu.make_async_copy(ckv_hbm.at[layer, pg], ckv_in.at[sl, p_i], sem.at[sl]))
            out.append(pltpu.make_async_copy(kpe_hbm.at[layer, pg], kpe_in.at[sl, p_i], sem.at[sl]))
        return out

    @pl.when(step == 0)
    def _():
        for cp in page_copies(b, c, slot):
            cp.start()

    @pl.when(step + 1 < pl.num_programs(0) * n_chunks)
    def _():
        wrap = c + 1 == n_chunks
        for cp in page_copies(jnp.where(wrap, b + 1, b), jnp.where(wrap, 0, c + 1), 1 - slot):
            cp.start()

    @pl.when(c == 0)
    def _():
        m_ref[...] = jnp.full_like(m_ref, -jnp.inf)
        l_ref[...] = jnp.zeros_like(l_ref)
        acc_ref[...] = jnp.zeros_like(acc_ref)
        kp_buf[rope:ROPE_GROUP, :] = jnp.zeros((ROPE_GROUP - rope, kp_buf.shape[1]), BF16)

    q = q_ref[0]
    q_lat = q[:, 0:kv_rank]
    q_pe = q[:, kv_rank:kv_rank + ROPE_GROUP]

    def update(s, v):
        m_prev = m_ref[...]
        m_new = jnp.maximum(m_prev, jnp.max(s, axis=-1, keepdims=True))
        alpha = jnp.exp(m_prev - m_new)
        p = jnp.exp(s - m_new)
        l_ref[...] = alpha * l_ref[...] + jnp.sum(p, axis=-1, keepdims=True)
        acc_ref[...] = alpha * acc_ref[...] + _dot(p.astype(BF16), v)
        m_ref[...] = m_new

    for cp in page_copies(b, c, slot):
        cp.wait()
    for p_i in range(n_pages_step):
        kc_buf[p_i * page:(p_i + 1) * page, :] = ckv_in[slot, p_i].astype(BF16)
        kp_buf[0:rope, p_i * page:(p_i + 1) * page] = kpe_in[slot, p_i].astype(BF16)
    kc = kc_buf[...]
    s = (_dot_nt(q_lat, kc) + _dot(q_pe, kp_buf[...])) * scale
    update(s, kc)

    @pl.when(c == pl.num_programs(1) - 1)
    def _():
        kn = kn_ref[0]
        kn_lat = kn[:, 0:kv_rank]
        s_new = (_dot_nt(q_lat, kn_lat) + _dot_nt(q_pe, kn[:, kv_rank:kv_rank + ROPE_GROUP])) * scale
        rows = q.shape[0]
        q_t = lax.broadcasted_iota(jnp.int32, (rows, 1), 0) % dec_seq
        k_t = lax.broadcasted_iota(jnp.int32, (1, kn.shape[0]), 1)
        update(jnp.where(k_t <= q_t, s_new, -jnp.inf), kn_lat)
        o_ref[0] = (acc_ref[...] / l_ref[...]).astype(o_ref.dtype)


def sample_attention(q_s, k_new, cache_ckv, cache_kpe_t, page_table, layer, scale, pages_per_step=16):
    n_seq, rows, dq = q_s.shape
    dec_seq = k_new.shape[1]
    k_new = jnp.pad(k_new, ((0, 0), (0, LANES - dec_seq), (0, 0)))
    page, kv_rank = cache_ckv.shape[2], cache_ckv.shape[3]
    rope = cache_kpe_t.shape[2]
    n_pages = page_table.shape[1]
    pps = _tile(n_pages, pages_per_step, 1)
    kern = functools.partial(_sample_attn_kernel, layer=layer, n_pages_step=pps, page=page, kv_rank=kv_rank,
                             rope=rope, dec_seq=dec_seq, scale=scale)
    grid_spec = pltpu.PrefetchScalarGridSpec(
        num_scalar_prefetch=1,
        grid=(n_seq, n_pages // pps),
        in_specs=[pl.BlockSpec((1, rows, dq), lambda b, c, pt: (b, 0, 0)),
                  pl.BlockSpec((1, LANES, dq), lambda b, c, pt: (b, 0, 0)),
                  pl.BlockSpec(memory_space=pl.ANY),
                  pl.BlockSpec(memory_space=pl.ANY)],
        out_specs=pl.BlockSpec((1, rows, kv_rank), lambda b, c, pt: (b, 0, 0)),
        scratch_shapes=[pltpu.VMEM((2, pps, page, kv_rank), F32),
                        pltpu.VMEM((2, pps, rope, page), F32),
                        pltpu.VMEM((pps * page, kv_rank), BF16),
                        pltpu.VMEM((ROPE_GROUP, pps * page), BF16),
                        pltpu.VMEM((rows, 1), F32), pltpu.VMEM((rows, 1), F32),
                        pltpu.VMEM((rows, kv_rank), F32),
                        pltpu.SemaphoreType.DMA((2,))])
    return pl.pallas_call(
        kern,
        grid_spec=grid_spec,
        out_shape=jax.ShapeDtypeStruct((n_seq, rows, kv_rank), BF16),
        compiler_params=_params(("arbitrary", "arbitrary")),
        name="sample_attention",
    )(page_table, q_s, k_new, cache_ckv, cache_kpe_t)


def _uv_proj_kernel(o_ref, w_ref, out_ref):
    out_ref[...] = _dot(o_ref[0], w_ref[0]).astype(out_ref.dtype)


def uv_proj(o_lat_hm, wuv):
    n_heads, n, kv_rank = o_lat_hm.shape
    vh = wuv.shape[2]
    return pl.pallas_call(
        _uv_proj_kernel,
        grid=(n_heads,),
        in_specs=[pl.BlockSpec((1, n, kv_rank), lambda h: (h, 0, 0)),
                  pl.BlockSpec((1, kv_rank, vh), lambda h: (h, 0, 0))],
        out_specs=pl.BlockSpec((n, vh), lambda h: (0, h)),
        out_shape=jax.ShapeDtypeStruct((n, n_heads * vh), BF16),
        compiler_params=_params(("parallel",)),
        name="uv_proj",
    )(o_lat_hm, wuv)


def _lru_gates(xc, wra, bra, wri, bri, lam):
    xb = xc.astype(BF16)
    r = jax.nn.sigmoid(_dot(xb, wra) + bra)
    i = jax.nn.sigmoid(_dot(xb, wri) + bri)
    log_a = -LRU_C * r * jax.nn.softplus(-lam)
    a = jnp.exp(log_a)
    one_minus_a2 = -jnp.tanh(log_a) * (a * a + 1.0)
    u = jnp.sqrt(one_minus_a2) * (i * xc)
    return a, u


def _rnn_gates_prompt_kernel(x_ref, cw_ref, cb_ref, wra_ref, bra_ref, wri_ref, bri_ref, lam_ref,
                             a_ref, u_ref, *, conv_w):
    x = x_ref[...]
    t_idx = lax.broadcasted_iota(jnp.int32, (x.shape[0], 1), 0)
    cw = cw_ref[...]
    acc = None
    for k in range(conv_w):
        d = conv_w - 1 - k
        xs = x if d == 0 else jnp.where(t_idx >= d, pltpu.roll(x, d, 0), 0.0)
        term = xs * cw[k:k + 1, :]
        acc = term if acc is None else acc + term
    xc = cb_ref[...] + acc
    a, u = _lru_gates(xc, wra_ref[0], bra_ref[...], wri_ref[0], bri_ref[...], lam_ref[...])
    a_ref[...] = a
    u_ref[...] = u


def rnn_gates_prompt(z, lay, batch, seq, conv_w, conv_b, wra, bra, wri, bri, lam):
    d_rnn = lay["d_rnn"]
    nb, cb = wra.shape[0], wra.shape[1]
    assert cb == LANES
    x_off = lay["off_xr"] // cb
    kern = functools.partial(_rnn_gates_prompt_kernel, conv_w=conv_w.shape[0])
    vec = lambda b, n: (0, n)
    blk = lambda b, n: (n, 0, 0)
    a, u = pl.pallas_call(
        kern,
        grid=(batch, nb),
        in_specs=[pl.BlockSpec((seq, cb), lambda b, n: (b, x_off + n)),
                  pl.BlockSpec((conv_w.shape[0], cb), vec),
                  pl.BlockSpec((1, cb), vec),
                  pl.BlockSpec((1, cb, cb), blk), pl.BlockSpec((1, cb), vec),
                  pl.BlockSpec((1, cb, cb), blk), pl.BlockSpec((1, cb), vec),
                  pl.BlockSpec((1, cb), vec)],
        out_specs=[pl.BlockSpec((seq, cb), lambda b, n: (b, n))] * 2,
        out_shape=[jax.ShapeDtypeStruct((batch * seq, d_rnn), F32)] * 2,
        compiler_params=_params(("parallel", "parallel")),
        name="rnn_gates_prompt",
    )(z, conv_w, conv_b.reshape(1, d_rnn), wra, bra.reshape(1, d_rnn), wri, bri.reshape(1, d_rnn),
      lam.reshape(1, d_rnn))
    return a, u


def _rnn_scan_prompt_kernel(a_ref, u_ref, g_ref, o_ref, hl_ref, h_ref, hs_ref):
    tt = pl.program_id(1)

    @pl.when(tt == 0)
    def _():
        h_ref[...] = jnp.zeros_like(h_ref)

    def step(t, h):
        h = a_ref[t] * h + u_ref[t]
        hs_ref[t] = h
        return h

    h = lax.fori_loop(0, a_ref.shape[0], step, h_ref[...], unroll=8)
    h_ref[...] = h
    o_ref[...] = (hs_ref[...] * jax.nn.gelu(g_ref[...])).astype(o_ref.dtype)

    @pl.when(tt == pl.num_programs(1) - 1)
    def _():
        hl_ref[0] = h


def rnn_scan_prompt(a, u, z, lay, batch, seq):
    d_rnn = lay["d_rnn"]
    nb = d_rnn // LANES
    assert lay["off_gr"] % d_rnn == 0 and z.shape[1] % LANES == 0
    tc = _tile(seq, 512, 8)
    nt = seq // tc
    a4 = a.reshape(batch * seq, nb, LANES)
    u4 = u.reshape(batch * seq, nb, LANES)
    z4 = z.reshape(z.shape[0], z.shape[1] // LANES, LANES)
    g_blk = lay["off_gr"] // d_rnn
    row = lambda b, t: (b * nt + t, 0, 0)
    o, hl = pl.pallas_call(
        _rnn_scan_prompt_kernel,
        grid=(batch, nt),
        in_specs=[pl.BlockSpec((tc, nb, LANES), row),
                  pl.BlockSpec((tc, nb, LANES), row),
                  pl.BlockSpec((tc, nb, LANES), lambda b, t: (b * nt + t, g_blk, 0))],
        out_specs=[pl.BlockSpec((tc, nb, LANES), row),
                   pl.BlockSpec((1, nb, LANES), lambda b, t: (b, 0, 0))],
        out_shape=[jax.ShapeDtypeStruct((batch * seq, nb, LANES), BF16),
                   jax.ShapeDtypeStruct((batch, nb, LANES), F32)],
        scratch_shapes=[pltpu.VMEM((nb, LANES), F32), pltpu.VMEM((tc, nb, LANES), F32)],
        compiler_params=_params(("parallel", "arbitrary")),
        name="rnn_scan_prompt",
    )(a4, u4, z4)
    return o.reshape(batch * seq, d_rnn), hl.reshape(batch, d_rnn)


def _rnn_sample_kernel(x_ref, st_ref, g_ref, h0_ref, cw_ref, cb_ref, wra_ref, bra_ref, wri_ref, bri_ref,
                       lam_ref, o_ref, hl_ref, *, conv_w):
    dec_seq, n_seq = x_ref.shape[0], x_ref.shape[1]
    cw = cw_ref[...]
    xp = [st_ref[k] for k in range(conv_w - 1)] + [x_ref[t] for t in range(dec_seq)]
    xc = []
    for t in range(dec_seq):
        acc = None
        for k in range(conv_w):
            term = xp[t + k] * cw[k:k + 1, :]
            acc = term if acc is None else acc + term
        xc.append(cb_ref[...] + acc)
    a, u = _lru_gates(jnp.concatenate(xc, axis=0), wra_ref[0], bra_ref[...], wri_ref[0], bri_ref[...],
                      lam_ref[...])
    h = h0_ref[...]
    for t in range(dec_seq):
        h = a[t * n_seq:(t + 1) * n_seq] * h + u[t * n_seq:(t + 1) * n_seq]
        o_ref[t] = (h * jax.nn.gelu(g_ref[t])).astype(o_ref.dtype)
    hl_ref[...] = h


def rnn_sample(x_t, st_t, g_t, h0, conv_w, conv_b, wra, bra, wri, bri, lam):
    dec_seq, n_seq, d_rnn = x_t.shape
    nb, cb = wra.shape[0], wra.shape[1]
    cwid = conv_w.shape[0]
    kern = functools.partial(_rnn_sample_kernel, conv_w=cwid)
    vec = lambda n: (0, n)
    blk = lambda n: (n, 0, 0)
    t3 = lambda n: (0, 0, n)
    return pl.pallas_call(
        kern,
        grid=(nb,),
        in_specs=[pl.BlockSpec((dec_seq, n_seq, cb), t3),
                  pl.BlockSpec((cwid - 1, n_seq, cb), t3),
                  pl.BlockSpec((dec_seq, n_seq, cb), t3),
                  pl.BlockSpec((n_seq, cb), vec),
                  pl.BlockSpec((cwid, cb), vec), pl.BlockSpec((1, cb), vec),
                  pl.BlockSpec((1, cb, cb), blk), pl.BlockSpec((1, cb), vec),
                  pl.BlockSpec((1, cb, cb), blk), pl.BlockSpec((1, cb), vec),
                  pl.BlockSpec((1, cb), vec)],
        out_specs=[pl.BlockSpec((dec_seq, n_seq, cb), t3),
                   pl.BlockSpec((n_seq, cb), vec)],
        out_shape=[jax.ShapeDtypeStruct((dec_seq, n_seq, d_rnn), BF16),
                   jax.ShapeDtypeStruct((n_seq, d_rnn), F32)],
        compiler_params=_params(("parallel",)),
        name="rnn_sample",
    )(x_t, st_t, g_t, h0, conv_w, conv_b.reshape(1, d_rnn), wra, bra.reshape(1, d_rnn),
      wri, bri.reshape(1, d_rnn), lam.reshape(1, d_rnn))


def _z_layout(q_rank, kv_rank, rope, d_rnn):
    off_kpe = q_rank + kv_rank
    off_xr = -(-(off_kpe + ROPE_GROUP) // d_rnn) * d_rnn
    off_gr = off_xr + d_rnn
    return dict(q_rank=q_rank, kv_rank=kv_rank, rope=rope, d_rnn=d_rnn, off_kpe=off_kpe,
                off_xr=off_xr, off_gr=off_gr, width=off_gr + d_rnn)


def _dup_rope_cols(w, rope):
    return jnp.concatenate([w, w], axis=-1)


def _prep_layer_weights(lay, w_in, w_uq, w_uk, w_uv, w_ra, w_ri, w_o, w_gate, w_up, w_down, d_ff_pad):
    q_rank, kv_rank, rope, d_rnn = lay["q_rank"], lay["kv_rank"], lay["rope"], lay["d_rnn"]
    d_model = w_in.shape[0]
    n_heads, nope = w_uk.shape[1], w_uk.shape[2]
    o = q_rank + kv_rank
    pad = lay["off_xr"] - (lay["off_kpe"] + ROPE_GROUP)
    w_in_p = jnp.concatenate(
        [w_in[:, :o], _dup_rope_cols(w_in[:, o:o + rope], rope), jnp.zeros((d_model, pad), w_in.dtype),
         w_in[:, o + rope:]], axis=1).astype(BF16)
    wq = w_uq.reshape(q_rank, n_heads, nope + rope)
    wq_nope = wq[:, :, :nope].reshape(q_rank, n_heads * nope).astype(BF16)
    wq_pe = _dup_rope_cols(wq[:, :, nope:], rope).reshape(q_rank, n_heads * ROPE_GROUP).astype(BF16)
    wuk_t = jnp.transpose(w_uk, (1, 2, 0)).astype(BF16)
    wuv_h = jnp.transpose(w_uv, (1, 0, 2)).astype(BF16)
    d_ff = w_gate.shape[1]
    fpad = d_ff_pad - d_ff
    wg = jnp.pad(w_gate, ((0, 0), (0, fpad))).astype(BF16)
    wu = jnp.pad(w_up, ((0, 0), (0, fpad))).astype(BF16)
    wd = jnp.pad(w_down, ((0, fpad), (0, 0))).astype(BF16)
    return dict(w_in=w_in_p, wq_nope=wq_nope, wq_pe=wq_pe, wuk_t=wuk_t, wuv=wuv_h,
                w_ra=w_ra.astype(BF16), w_ri=w_ri.astype(BF16), w_o=w_o.astype(BF16), wg=wg, wu=wu, wd=wd)


def _rope_tables(pos, rope):
    half = rope // 2
    inv = ROPE_THETA ** (-jnp.arange(half, dtype=F32) / half)
    ang = pos.astype(F32)[:, None] * inv[None, :]
    c, s = jnp.cos(ang), jnp.sin(ang)
    zero = jnp.zeros_like(c)
    return (jnp.concatenate([c, c, zero, zero], axis=1),
            jnp.concatenate([-s, s, zero, zero], axis=1))


def kernel(x_prompt, x_sample, cache_ckv, cache_kpe, state_h, state_conv, page_table, norm_mix, w_in, q_norm, kv_norm, w_uq, w_uk, w_uv, conv_w, conv_b, w_ra, b_ra, w_ri, b_ri, lru_lambda, w_o, norm_ffn, w_gate, w_up, w_down, norm_final):
    batch, seq, d_model = x_prompt.shape
    n_seq, dec_seq, _ = x_sample.shape
    depth = norm_mix.shape[0]
    q_rank, kv_rank = q_norm.shape[1], kv_norm.shape[1]
    rope = cache_kpe.shape[3]
    n_heads, nope = w_uk.shape[2], w_uk.shape[3]
    d_rnn = state_h.shape[2]
    cwid = conv_w.shape[1]
    d_ff = w_gate.shape[2]
    past_len = page_table.shape[1] * cache_ckv.shape[2]
    assert 4 * (rope // 2) == ROPE_GROUP and dec_seq >= cwid - 1 and seq >= cwid - 1
    scale = float(nope + rope) ** -0.5
    lay = _z_layout(q_rank, kv_rank, rope, d_rnn)
    d_ff_pad = -(-d_ff // 1024) * 1024 if d_ff > 1024 else d_ff
    tq = 128

    n_p, n_s = batch * seq, n_seq * dec_seq
    nt = n_p + n_s
    assert n_p % tq == 0 and n_s % tq == 0 and tq % dec_seq == 0
    pos = jnp.concatenate([jnp.tile(jnp.arange(seq), batch), jnp.tile(past_len + jnp.arange(dec_seq), n_seq)])
    cs, sn = _rope_tables(pos, rope)
    cache_kpe_t = jnp.swapaxes(cache_kpe, 2, 3)

    x = jnp.concatenate([x_prompt.reshape(n_p, d_model), x_sample.reshape(n_s, d_model)], axis=0)
    outs = {k: [] for k in ("p_ckv", "p_kpe", "p_h", "p_conv", "s_ckv", "s_kpe", "s_h", "s_conv")}
    for l in range(depth):
        w = _prep_layer_weights(lay, w_in[l], w_uq[l], w_uk[l], w_uv[l], w_ra[l], w_ri[l], w_o[l],
                                w_gate[l], w_up[l], w_down[l], d_ff_pad)
        h = rmsnorm(x, norm_mix[l], BF16)
        z = matmul(h, w["w_in"], F32)
        qcat, ckv, kpe, kcat = mla_prep(z, lay, cs, sn, q_norm[l], kv_norm[l],
                                        w["wq_nope"], w["wq_pe"], w["wuk_t"], tq)
        dq = kcat.shape[1]
        o_attn_p = prompt_attention(qcat, kcat, w["wuv"], batch, seq, kv_rank, scale)
        seq_per_tile = tq // dec_seq
        q_s = qcat[n_p // tq:].reshape(n_s // tq, n_heads, seq_per_tile, dec_seq, dq)
        q_s = jnp.transpose(q_s, (0, 2, 1, 3, 4)).reshape(n_seq, n_heads * dec_seq, dq)
        k_new = kcat[n_p:].reshape(n_seq, dec_seq, dq)
        o_lat_s = sample_attention(q_s, k_new, cache_ckv, cache_kpe_t, page_table, l, scale)
        o_lat_s = jnp.transpose(o_lat_s.reshape(n_seq, n_heads, dec_seq, kv_rank), (1, 0, 2, 3))
        o_attn_s = uv_proj(o_lat_s.reshape(n_heads, n_s, kv_rank), w["wuv"])
        a_p, u_p = rnn_gates_prompt(z, lay, batch, seq, conv_w[l], conv_b[l], w["w_ra"], b_ra[l],
                                    w["w_ri"], b_ri[l], lru_lambda[l])
        o_rnn_p, h_last_p = rnn_scan_prompt(a_p, u_p, z, lay, batch, seq)
        xr_p = z[:n_p, lay["off_xr"]:lay["off_xr"] + d_rnn].reshape(batch, seq, d_rnn)
        z_s = z[n_p:].reshape(n_seq, dec_seq, -1)
        xr_s = z_s[:, :, lay["off_xr"]:lay["off_xr"] + d_rnn]
        gr_s = z_s[:, :, lay["off_gr"]:lay["off_gr"] + d_rnn]
        o_rnn_s, h_last_s = rnn_sample(jnp.swapaxes(xr_s, 0, 1), jnp.swapaxes(state_conv[l], 0, 1),
                                       jnp.swapaxes(gr_s, 0, 1), state_h[l], conv_w[l], conv_b[l],
                                       w["w_ra"], b_ra[l], w["w_ri"], b_ri[l], lru_lambda[l])
        o_rnn_s = jnp.swapaxes(o_rnn_s, 0, 1).reshape(n_s, d_rnn)
        mix = jnp.concatenate([jnp.concatenate([o_attn_p, o_attn_s], axis=0),
                               jnp.concatenate([o_rnn_p, o_rnn_s], axis=0)], axis=1)
        x = matmul_res(mix, w["w_o"], x)
        f = rmsnorm(x, norm_ffn[l], BF16)
        x = matmul_kres(swiglu(f, w["wg"], w["wu"]), w["wd"], x)

        outs["p_ckv"].append(ckv[:n_p].reshape(batch, seq, kv_rank))
        outs["p_kpe"].append(kpe[:n_p].reshape(batch, seq, rope))
        outs["p_h"].append(h_last_p)
        outs["p_conv"].append(xr_p[:, seq - (cwid - 1):])
        outs["s_ckv"].append(ckv[n_p:].reshape(n_seq, dec_seq, kv_rank))
        outs["s_kpe"].append(kpe[n_p:].reshape(n_seq, dec_seq, rope))
        outs["s_h"].append(h_last_s)
        outs["s_conv"].append(xr_s[:, dec_seq - (cwid - 1):])

    y_prompt = rmsnorm(x, norm_final, F32, 0, n_p).reshape(batch, seq, d_model)
    y_sample = rmsnorm(x, norm_final, F32, n_p, n_s).reshape(n_seq, dec_seq, d_model)
    st = lambda k: jnp.stack(outs[k])
    return (y_prompt, y_sample, st("p_ckv"), st("p_kpe"), st("p_h"), st("p_conv"),
            st("s_ckv"), st("s_kpe"), st("s_h"), st("s_conv"))
```

```python
import functools

import jax
import jax.numpy as jnp
from jax import lax
from jax.experimental import pallas as pl
from jax.experimental.pallas import tpu as pltpu

EPS = 1e-6
ROPE_THETA = 10000.0
LRU_C = 8.0
LOG2E = 1.4426950408889634

LANES = 128
ROPE_GROUP = 128
VMEM_LIMIT = 56 * 1024 * 1024

F32 = jnp.float32
BF16 = jnp.bfloat16


def _tile(n, pref, mult):
    best = None
    for t in range(mult, min(n, pref) + 1, mult):
        if n % t == 0:
            best = t
    return best if best is not None else n


def _params(sem):
    return pltpu.CompilerParams(dimension_semantics=sem, vmem_limit_bytes=VMEM_LIMIT)


def _dot(a, b):
    return jnp.dot(a, b, preferred_element_type=F32)


def _dot_nt(a, b):
    return lax.dot_general(a, b, (((1,), (1,)), ((), ())), preferred_element_type=F32)


def _rmsnorm_kernel(x_ref, g_ref, o_ref):
    x = x_ref[...]
    y = x * lax.rsqrt(jnp.mean(x * x, axis=-1, keepdims=True) + EPS)
    o_ref[...] = (y * g_ref[...]).astype(o_ref.dtype)


def rmsnorm(x, g, out_dtype, row0=0, rows=None):
    m, d = x.shape
    rows = m if rows is None else rows
    tm = _tile(rows, 512, 16)
    assert row0 % tm == 0
    off = row0 // tm
    return pl.pallas_call(
        _rmsnorm_kernel,
        grid=(rows // tm,),
        in_specs=[pl.BlockSpec((tm, d), lambda i: (i + off, 0)),
                  pl.BlockSpec((1, d), lambda i: (0, 0))],
        out_specs=pl.BlockSpec((tm, d), lambda i: (i, 0)),
        out_shape=jax.ShapeDtypeStruct((rows, d), out_dtype),
        compiler_params=_params(("parallel",)),
        name="rmsnorm",
    )(x, g.reshape(1, d))


def _mm_kernel(x_ref, w_ref, o_ref):
    o_ref[...] = _dot(x_ref[...], w_ref[...]).astype(o_ref.dtype)


def _mm_res2_kernel(xa_ref, xb_ref, wa_ref, wb_ref, r_ref, o_ref):
    o_ref[...] = r_ref[...] + (_dot(xa_ref[...], wa_ref[...]) + _dot(xb_ref[...], wb_ref[...]))


def _swiglu_kernel(x_ref, wg_ref, wu_ref, o_ref):
    x = x_ref[...]
    g = _dot(x, wg_ref[...])
    u = _dot(x, wu_ref[...])
    o_ref[...] = (jax.nn.silu(g) * u).astype(o_ref.dtype)


def _mm_kres_kernel(x_ref, w_ref, r_ref, o_ref, acc_ref):
    k = pl.program_id(2)

    @pl.when(k == 0)
    def _():
        acc_ref[...] = jnp.zeros_like(acc_ref)

    acc_ref[...] += _dot(x_ref[...], w_ref[...])

    @pl.when(k == pl.num_programs(2) - 1)
    def _():
        o_ref[...] = r_ref[...] + acc_ref[...]


def matmul(x, w, layer, out_dtype, tm_pref=1088, tn_pref=1024):
    m, k = x.shape
    n = w.shape[2]
    tm, tn = _tile(m, tm_pref, 16), _tile(n, tn_pref, LANES)
    return pl.pallas_call(
        _mm_kernel,
        grid=(m // tm, n // tn),
        in_specs=[pl.BlockSpec((tm, k), lambda i, j: (i, 0)),
                  pl.BlockSpec((None, k, tn), lambda i, j: (layer, 0, j))],
        out_specs=pl.BlockSpec((tm, tn), lambda i, j: (i, j)),
        out_shape=jax.ShapeDtypeStruct((m, n), out_dtype),
        compiler_params=_params(("parallel", "parallel")),
        name="matmul",
    )(x, w)


def matmul_res2(xa, xb, w, layer, res, tm_pref=1088, tn_pref=512):
    m, ka = xa.shape
    kb = xb.shape[1]
    n = w.shape[2]
    assert ka == kb and w.shape[1] == ka + kb
    tm, tn = _tile(m, tm_pref, 16), _tile(n, tn_pref, LANES)
    return pl.pallas_call(
        _mm_res2_kernel,
        grid=(m // tm, n // tn),
        in_specs=[pl.BlockSpec((tm, ka), lambda i, j: (i, 0)),
                  pl.BlockSpec((tm, kb), lambda i, j: (i, 0)),
                  pl.BlockSpec((None, ka, tn), lambda i, j: (layer, 0, j)),
                  pl.BlockSpec((None, kb, tn), lambda i, j: (layer, 1, j)),
                  pl.BlockSpec((tm, tn), lambda i, j: (i, j))],
        out_specs=pl.BlockSpec((tm, tn), lambda i, j: (i, j)),
        out_shape=jax.ShapeDtypeStruct((m, n), F32),
        compiler_params=_params(("parallel", "parallel")),
        name="matmul_res2",
    )(xa, xb, w, w, res)


def swiglu(x, wg, wu, layer, tm_pref=1088, tn_pref=512):
    m, k = x.shape
    n = wg.shape[2]
    tm, tn = _tile(m, tm_pref, 16), _tile(n, tn_pref, LANES)
    w_spec = pl.BlockSpec((None, k, tn), lambda i, j: (layer, 0, j))
    return pl.pallas_call(
        _swiglu_kernel,
        grid=(m // tm, n // tn),
        in_specs=[pl.BlockSpec((tm, k), lambda i, j: (i, 0)), w_spec, w_spec],
        out_specs=pl.BlockSpec((tm, tn), lambda i, j: (i, j)),
        out_shape=jax.ShapeDtypeStruct((m, n), BF16),
        compiler_params=_params(("parallel", "parallel")),
        name="swiglu",
    )(x, wg, wu)


def matmul_kres(x, w, layer, res, tm_pref=1088, tn_pref=512, tk_pref=5504):
    m = x.shape[0]
    k, n = w.shape[1], w.shape[2]
    assert x.shape[1] >= k
    tm, tn, tk = _tile(m, tm_pref, 16), _tile(n, tn_pref, LANES), _tile(k, tk_pref, LANES)
    return pl.pallas_call(
        _mm_kres_kernel,
        grid=(m // tm, n // tn, k // tk),
        in_specs=[pl.BlockSpec((tm, tk), lambda i, j, kk: (i, kk)),
                  pl.BlockSpec((None, tk, tn), lambda i, j, kk: (layer, kk, j)),
                  pl.BlockSpec((tm, tn), lambda i, j, kk: (i, j))],
        out_specs=pl.BlockSpec((tm, tn), lambda i, j, kk: (i, j)),
        out_shape=jax.ShapeDtypeStruct((m, n), F32),
        scratch_shapes=[pltpu.VMEM((tm, tn), F32)],
        compiler_params=_params(("parallel", "parallel", "arbitrary")),
        name="matmul_kres",
    )(x, w, res)


def _rope_group(x, cs, sn):
    return x * cs + pltpu.roll(x, ROPE_GROUP // 4, 1) * sn


def _mla_prep_kernel(qc_ref, kvc_ref, kpe_ref, cs_ref, sn_ref, qg_ref, kvg_ref,
                     wqn_ref, wqp_ref, wuk_ref,
                     qcat_ref, ckv_ref, kpeo_ref, kcat_ref, *, n_heads, nope, kv_rank, rope):
    tq = qcat_ref.shape[2]
    n_sub = qcat_ref.shape[0]
    cs = cs_ref[...]
    sn = sn_ref[...]

    qc = qc_ref[...]
    qn = (qc * lax.rsqrt(jnp.mean(qc * qc, axis=-1, keepdims=True) + EPS) * qg_ref[...]).astype(BF16)
    q_nope = _dot(qn, wqn_ref[...])
    q_pe = _dot(qn, wqp_ref[...])
    for h in range(n_heads):
        ql = _dot(q_nope[:, h * nope:(h + 1) * nope].astype(BF16), wuk_ref[h]).astype(BF16)
        rp = _rope_group(q_pe[:, h * ROPE_GROUP:(h + 1) * ROPE_GROUP], cs, sn).astype(BF16)
        for t in range(n_sub):
            qcat_ref[t, h, :, 0:kv_rank] = ql[t * tq:(t + 1) * tq]
            qcat_ref[t, h, :, kv_rank:kv_rank + ROPE_GROUP] = rp[t * tq:(t + 1) * tq]

    kvc = kvc_ref[...]
    ckv = kvc * lax.rsqrt(jnp.mean(kvc * kvc, axis=-1, keepdims=True) + EPS) * kvg_ref[...]
    kr = _rope_group(kpe_ref[...], cs, sn)
    ckv_ref[...] = ckv
    kpeo_ref[...] = kr[:, 0:rope]
    kcat_ref[:, 0:kv_rank] = ckv.astype(BF16)
    kcat_ref[:, kv_rank:kv_rank + ROPE_GROUP] = kr.astype(BF16)


def mla_prep(z, lay, cs, sn, q_norm, kv_norm, wq_nope, wq_pe, wuk_t, layer, tq):
    nt = z.shape[0]
    qr, kv, rope = lay["q_rank"], lay["kv_rank"], lay["rope"]
    n_heads, nope = wuk_t.shape[1], wuk_t.shape[2]
    dq = kv + ROPE_GROUP
    tm = 2 * tq if nt % (2 * tq) == 0 else tq
    n_sub = tm // tq
    assert nt % tm == 0 and qr % kv == 0 and lay["off_kpe"] % ROPE_GROUP == 0
    kern = functools.partial(_mla_prep_kernel, n_heads=n_heads, nope=nope, kv_rank=kv, rope=rope)
    const2 = lambda i: (0, 0)
    return pl.pallas_call(
        kern,
        grid=(nt // tm,),
        in_specs=[pl.BlockSpec((tm, qr), lambda i: (i, 0)),
                  pl.BlockSpec((tm, kv), lambda i: (i, qr // kv)),
                  pl.BlockSpec((tm, ROPE_GROUP), lambda i: (i, lay["off_kpe"] // ROPE_GROUP)),
                  pl.BlockSpec((tm, ROPE_GROUP), lambda i: (i, 0)),
                  pl.BlockSpec((tm, ROPE_GROUP), lambda i: (i, 0)),
                  pl.BlockSpec((1, qr), const2),
                  pl.BlockSpec((1, kv), const2),
                  pl.BlockSpec((None,) + wq_nope.shape[1:], lambda i: (layer, 0, 0)),
                  pl.BlockSpec((None,) + wq_pe.shape[1:], lambda i: (layer, 0, 0)),
                  pl.BlockSpec((None,) + wuk_t.shape[1:], lambda i: (layer, 0, 0, 0))],
        out_specs=[pl.BlockSpec((n_sub, n_heads, tq, dq), lambda i: (i, 0, 0, 0)),
                   pl.BlockSpec((tm, kv), lambda i: (i, 0)),
                   pl.BlockSpec((tm, rope), lambda i: (i, 0)),
                   pl.BlockSpec((tm, dq), lambda i: (i, 0))],
        out_shape=[jax.ShapeDtypeStruct((nt // tq, n_heads, tq, dq), BF16),
                   jax.ShapeDtypeStruct((nt, kv), F32),
                   jax.ShapeDtypeStruct((nt, rope), F32),
                   jax.ShapeDtypeStruct((nt, dq), BF16)],
        compiler_params=_params(("parallel",)),
        name="mla_prep",
    )(z, z, z, cs, sn, q_norm.reshape(1, qr), kv_norm.reshape(1, kv), wq_nope, wq_pe, wuk_t)


def _prompt_attn_kernel(q_ref, k_ref, wuv_ref, o_ref, m_ref, l_ref, acc_ref, s_ref, p_ref, alpha_ref,
                        *, tq, tk, kv_rank, scale, chunk_rows, unroll):
    i = pl.program_id(1)
    j = pl.program_id(2)
    n_heads = q_ref.shape[1]
    rows = n_heads * tq

    @pl.when(j == 0)
    def _():
        m_ref[...] = jnp.full_like(m_ref, -jnp.inf)
        l_ref[...] = jnp.zeros_like(l_ref)
        acc_ref[...] = jnp.zeros_like(acc_ref)

    last = ((i + 1) * tq - 1) // tk
    c2 = scale * LOG2E

    def block(masked):
        k = k_ref[...]
        s_ref[...] = _dot_nt(q_ref[0].reshape(rows, q_ref.shape[3]), k)

        def softmax_chunk(ci, carry):
            rs = pl.ds(pl.multiple_of(ci * chunk_rows, chunk_rows), chunk_rows)
            s = s_ref[rs, :]
            if masked:
                row = ci * chunk_rows + lax.broadcasted_iota(jnp.int32, (chunk_rows, 1), 0)
                q_pos = i * tq + (row & (tq - 1))
                k_pos = j * tk + lax.broadcasted_iota(jnp.int32, (1, tk), 1)
                s = jnp.where(k_pos <= q_pos, s, -jnp.inf)
            m_prev = m_ref[rs, :]
            m_new = jnp.maximum(m_prev, jnp.max(s, axis=-1, keepdims=True))
            alpha = jnp.exp2((m_prev - m_new) * c2)
            p = jnp.exp2((s - m_new) * c2)
            l_ref[rs, :] = alpha * l_ref[rs, :] + jnp.sum(p, axis=-1, keepdims=True)
            m_ref[rs, :] = m_new
            alpha_ref[rs, :] = alpha
            p_ref[rs, :] = p.astype(BF16)
            return carry

        lax.fori_loop(0, rows // chunk_rows, softmax_chunk, 0, unroll=unroll)
        acc_ref[...] = alpha_ref[...] * acc_ref[...] + _dot(p_ref[...], k[:, 0:kv_rank])

    pl.when(j < last)(functools.partial(block, False))
    pl.when(j == last)(functools.partial(block, True))

    @pl.when(j == pl.num_programs(2) - 1)
    def _():
        vh = wuv_ref.shape[2]
        for h in range(n_heads):
            sl = slice(h * tq, (h + 1) * tq)
            o_lat = (acc_ref[sl, :] / l_ref[sl, :]).astype(BF16)
            o_ref[:, h * vh:(h + 1) * vh] = _dot(o_lat, wuv_ref[h]).astype(o_ref.dtype)


def prompt_attention(qcat, kcat, wuv, layer, batch, seq, kv_rank, scale):
    n_heads, tq, dq = qcat.shape[1], qcat.shape[2], qcat.shape[3]
    assert tq & (tq - 1) == 0 and seq % tq == 0
    tk = _tile(seq, 512, tq)
    nq, nk = seq // tq, seq // tk
    vh = wuv.shape[3]
    rows = n_heads * tq
    chunk_rows = 128
    kern = functools.partial(_prompt_attn_kernel, tq=tq, tk=tk, kv_rank=kv_rank, scale=scale,
                             chunk_rows=chunk_rows, unroll=min(4, rows // chunk_rows))

    def k_map(b, i, j):
        return (b * nk + jnp.minimum(j, ((i + 1) * tq - 1) // tk), 0)

    return pl.pallas_call(
        kern,
        grid=(batch, nq, nk),
        in_specs=[pl.BlockSpec((1, n_heads, tq, dq), lambda b, i, j: (b * nq + i, 0, 0, 0)),
                  pl.BlockSpec((tk, dq), k_map),
                  pl.BlockSpec((None,) + wuv.shape[1:], lambda b, i, j: (layer, 0, 0, 0))],
        out_specs=pl.BlockSpec((tq, n_heads * vh), lambda b, i, j: (b * nq + i, 0)),
        out_shape=jax.ShapeDtypeStruct((batch * seq, n_heads * vh), BF16),
        scratch_shapes=[pltpu.VMEM((rows, 1), F32), pltpu.VMEM((rows, 1), F32),
                        pltpu.VMEM((rows, kv_rank), F32),
                        pltpu.VMEM((rows, tk), F32), pltpu.VMEM((rows, tk), BF16), pltpu.VMEM((rows, 1), F32)],
        compiler_params=_params(("parallel", "parallel", "arbitrary")),
        name="prompt_attention",
    )(qcat, kcat, wuv)


def _sample_attn_kernel(pt_ref, q_ref, kn_ref, ckv_hbm, kpe_hbm, o_ref,
                        ckv_in, kpe_in, kc_buf, kp_buf, m_ref, l_ref, acc_ref, sem,
                        *, layer, n_pages_step, page, kv_rank, rope, dec_seq, scale):
    b = pl.program_id(0)
    c = pl.program_id(1)
    n_chunks = pl.num_programs(1)
    step = b * n_chunks + c
    slot = step % 2

    def page_copies(bb, cc, sl):
        out = []
        for p_i in range(n_pages_step):
            pg = pt_ref[bb, cc * n_pages_step + p_i]
            out.append(pltpu.make_async_copy(ckv_hbm.at[layer, pg], ckv_in.at[sl, p_i], sem.at[sl]))
            out.append(pltpu.make_async_copy(kpe_hbm.at[layer, pg], kpe_in.at[sl, p_i], sem.at[sl]))
        return out

    def start_all(copies):
        for n, cp in enumerate(copies):
            cp.start(priority=(n // 2) % 2)

    @pl.when(step == 0)
    def _():
        start_all(page_copies(b, c, slot))

    @pl.when(step + 1 < pl.num_programs(0) * n_chunks)
    def _():
        wrap = c + 1 == n_chunks
        start_all(page_copies(jnp.where(wrap, b + 1, b), jnp.where(wrap, 0, c + 1), 1 - slot))

    @pl.when(c == 0)
    def _():
        m_ref[...] = jnp.full_like(m_ref, -jnp.inf)
        l_ref[...] = jnp.zeros_like(l_ref)
        acc_ref[...] = jnp.zeros_like(acc_ref)
        kp_buf[rope:ROPE_GROUP, :] = jnp.zeros((ROPE_GROUP - rope, kp_buf.shape[1]), BF16)

    q = q_ref[0]
    q_lat = q[:, 0:kv_rank]
    q_pe = q[:, kv_rank:kv_rank + ROPE_GROUP]

    def update(s, v):
        m_prev = m_ref[...]
        m_new = jnp.maximum(m_prev, jnp.max(s, axis=-1, keepdims=True))
        alpha = jnp.exp(m_prev - m_new)
        p = jnp.exp(s - m_new)
        l_ref[...] = alpha * l_ref[...] + jnp.sum(p, axis=-1, keepdims=True)
        acc_ref[...] = alpha * acc_ref[...] + _dot(p.astype(BF16), v)
        m_ref[...] = m_new

    for cp in page_copies(b, c, slot):
        cp.wait()
    for p_i in range(n_pages_step):
        kc_buf[p_i * page:(p_i + 1) * page, :] = ckv_in[slot, p_i].astype(BF16)
        kp_buf[0:rope, p_i * page:(p_i + 1) * page] = kpe_in[slot, p_i].astype(BF16)
    kc = kc_buf[...]
    s = (_dot_nt(q_lat, kc) + _dot(q_pe, kp_buf[...])) * scale
    update(s, kc)

    @pl.when(c == pl.num_programs(1) - 1)
    def _():
        kn = kn_ref[0]
        kn_lat = kn[:, 0:kv_rank]
        s_new = (_dot_nt(q_lat, kn_lat) + _dot_nt(q_pe, kn[:, kv_rank:kv_rank + ROPE_GROUP])) * scale
        rows = q.shape[0]
        q_t = lax.broadcasted_iota(jnp.int32, (rows, 1), 0) % dec_seq
        k_t = lax.broadcasted_iota(jnp.int32, (1, kn.shape[0]), 1)
        update(jnp.where(k_t <= q_t, s_new, -jnp.inf), kn_lat)
        o_ref[0] = (acc_ref[...] / l_ref[...]).astype(o_ref.dtype)


def sample_attention(q_s, k_new, cache_ckv, cache_kpe_t, page_table, layer, scale, pages_per_step=16):
    n_seq, rows, dq = q_s.shape
    dec_seq = k_new.shape[1]
    k_new = jnp.pad(k_new, ((0, 0), (0, LANES - dec_seq), (0, 0)))
    page, kv_rank = cache_ckv.shape[2], cache_ckv.shape[3]
    rope = cache_kpe_t.shape[2]
    n_pages = page_table.shape[1]
    pps = _tile(n_pages, pages_per_step, 1)
    kern = functools.partial(_sample_attn_kernel, layer=layer, n_pages_step=pps, page=page, kv_rank=kv_rank,
                             rope=rope, dec_seq=dec_seq, scale=scale)
    grid_spec = pltpu.PrefetchScalarGridSpec(
        num_scalar_prefetch=1,
        grid=(n_seq, n_pages // pps),
        in_specs=[pl.BlockSpec((1, rows, dq), lambda b, c, pt: (b, 0, 0)),
                  pl.BlockSpec((1, LANES, dq), lambda b, c, pt: (b, 0, 0)),
                  pl.BlockSpec(memory_space=pl.ANY),
                  pl.BlockSpec(memory_space=pl.ANY)],
        out_specs=pl.BlockSpec((1, rows, kv_rank), lambda b, c, pt: (b, 0, 0)),
        scratch_shapes=[pltpu.VMEM((2, pps, page, kv_rank), F32),
                        pltpu.VMEM((2, pps, rope, page), F32),
                        pltpu.VMEM((pps * page, kv_rank), BF16),
                        pltpu.VMEM((ROPE_GROUP, pps * page), BF16),
                        pltpu.VMEM((rows, 1), F32), pltpu.VMEM((rows, 1), F32),
                        pltpu.VMEM((rows, kv_rank), F32),
                        pltpu.SemaphoreType.DMA((2,))])
    return pl.pallas_call(
        kern,
        grid_spec=grid_spec,
        out_shape=jax.ShapeDtypeStruct((n_seq, rows, kv_rank), BF16),
        compiler_params=_params(("arbitrary", "arbitrary")),
        name="sample_attention",
    )(page_table, q_s, k_new, cache_ckv, cache_kpe_t)


def _uv_proj_kernel(o_ref, w_ref, out_ref):
    out_ref[...] = _dot(o_ref[0], w_ref[0]).astype(out_ref.dtype)


def uv_proj(o_lat_hm, wuv, layer):
    n_heads, n, kv_rank = o_lat_hm.shape
    vh = wuv.shape[3]
    return pl.pallas_call(
        _uv_proj_kernel,
        grid=(n_heads,),
        in_specs=[pl.BlockSpec((1, n, kv_rank), lambda h: (h, 0, 0)),
                  pl.BlockSpec((None, 1, kv_rank, vh), lambda h: (layer, h, 0, 0))],
        out_specs=pl.BlockSpec((n, vh), lambda h: (0, h)),
        out_shape=jax.ShapeDtypeStruct((n, n_heads * vh), BF16),
        compiler_params=_params(("parallel",)),
        name="uv_proj",
    )(o_lat_hm, wuv)


def _lru_gates(xc, wra, bra, wri, bri, lam):
    xb = xc.astype(BF16)
    r = jax.nn.sigmoid(_dot(xb, wra) + bra)
    i = jax.nn.sigmoid(_dot(xb, wri) + bri)
    log_a = -LRU_C * r * jax.nn.softplus(-lam)
    a = jnp.exp(log_a)
    one_minus_a2 = -jnp.tanh(log_a) * (a * a + 1.0)
    u = jnp.sqrt(one_minus_a2) * (i * xc)
    return a, u


def _rnn_gates_prompt_kernel(x_ref, cw_ref, cb_ref, wra_ref, bra_ref, wri_ref, bri_ref, lam_ref,
                             a_ref, u_ref, *, conv_w):
    x = x_ref[...]
    t_idx = lax.broadcasted_iota(jnp.int32, (x.shape[0], 1), 0)
    cw = cw_ref[...]
    acc = None
    for k in range(conv_w):
        d = conv_w - 1 - k
        xs = x if d == 0 else jnp.where(t_idx >= d, pltpu.roll(x, d, 0), 0.0)
        term = xs * cw[k:k + 1, :]
        acc = term if acc is None else acc + term
    xc = cb_ref[...] + acc
    a, u = _lru_gates(xc, wra_ref[0], bra_ref[...], wri_ref[0], bri_ref[...], lam_ref[...])
    a_ref[...] = a
    u_ref[...] = u


def rnn_gates_prompt(z, lay, batch, seq, conv_w, conv_b, wra, bra, wri, bri, lam):
    d_rnn = lay["d_rnn"]
    nb, cb = wra.shape[0], wra.shape[1]
    assert cb == LANES
    x_off = lay["off_xr"] // cb
    kern = functools.partial(_rnn_gates_prompt_kernel, conv_w=conv_w.shape[0])
    vec = lambda b, n: (0, n)
    blk = lambda b, n: (n, 0, 0)
    a, u = pl.pallas_call(
        kern,
        grid=(batch, nb),
        in_specs=[pl.BlockSpec((seq, cb), lambda b, n: (b, x_off + n)),
                  pl.BlockSpec((conv_w.shape[0], cb), vec),
                  pl.BlockSpec((1, cb), vec),
                  pl.BlockSpec((1, cb, cb), blk), pl.BlockSpec((1, cb), vec),
                  pl.BlockSpec((1, cb, cb), blk), pl.BlockSpec((1, cb), vec),
                  pl.BlockSpec((1, cb), vec)],
        out_specs=[pl.BlockSpec((seq, cb), lambda b, n: (b, n))] * 2,
        out_shape=[jax.ShapeDtypeStruct((batch * seq, d_rnn), F32)] * 2,
        compiler_params=_params(("parallel", "parallel")),
        name="rnn_gates_prompt",
    )(z, conv_w, conv_b.reshape(1, d_rnn), wra, bra.reshape(1, d_rnn), wri, bri.reshape(1, d_rnn),
      lam.reshape(1, d_rnn))
    return a, u


def _rnn_scan_prompt_kernel(a_ref, u_ref, g_ref, o_ref, hl_ref, h_ref, hs_ref):
    tt = pl.program_id(1)

    @pl.when(tt == 0)
    def _():
        h_ref[...] = jnp.zeros_like(h_ref)

    def step(t, h):
        h = a_ref[t] * h + u_ref[t]
        hs_ref[t] = h
        return h

    h = lax.fori_loop(0, a_ref.shape[0], step, h_ref[...], unroll=8)
    h_ref[...] = h
    o_ref[...] = (hs_ref[...] * jax.nn.gelu(g_ref[...])).astype(o_ref.dtype)

    @pl.when(tt == pl.num_programs(1) - 1)
    def _():
        hl_ref[0] = h


def rnn_scan_prompt(a, u, z, lay, batch, seq):
    d_rnn = lay["d_rnn"]
    nb = d_rnn // LANES
    assert lay["off_gr"] % d_rnn == 0 and z.shape[1] % LANES == 0
    tc = _tile(seq, 512, 8)
    nt = seq // tc
    a4 = a.reshape(batch * seq, nb, LANES)
    u4 = u.reshape(batch * seq, nb, LANES)
    z4 = z.reshape(z.shape[0], z.shape[1] // LANES, LANES)
    g_blk = lay["off_gr"] // d_rnn
    row = lambda b, t: (b * nt + t, 0, 0)
    o, hl = pl.pallas_call(
        _rnn_scan_prompt_kernel,
        grid=(batch, nt),
        in_specs=[pl.BlockSpec((tc, nb, LANES), row),
                  pl.BlockSpec((tc, nb, LANES), row),
                  pl.BlockSpec((tc, nb, LANES), lambda b, t: (b * nt + t, g_blk, 0))],
        out_specs=[pl.BlockSpec((tc, nb, LANES), row),
                   pl.BlockSpec((1, nb, LANES), lambda b, t: (b, 0, 0))],
        out_shape=[jax.ShapeDtypeStruct((batch * seq, nb, LANES), BF16),
                   jax.ShapeDtypeStruct((batch, nb, LANES), F32)],
        scratch_shapes=[pltpu.VMEM((nb, LANES), F32), pltpu.VMEM((tc, nb, LANES), F32)],
        compiler_params=_params(("parallel", "arbitrary")),
        name="rnn_scan_prompt",
    )(a4, u4, z4)
    return o.reshape(batch * seq, d_rnn), hl.reshape(batch, d_rnn)


def _rnn_sample_kernel(x_ref, st_ref, g_ref, h0_ref, cw_ref, cb_ref, wra_ref, bra_ref, wri_ref, bri_ref,
                       lam_ref, o_ref, hl_ref, *, conv_w):
    dec_seq, n_seq = x_ref.shape[0], x_ref.shape[1]
    cw = cw_ref[...]
    xp = [st_ref[k] for k in range(conv_w - 1)] + [x_ref[t] for t in range(dec_seq)]
    xc = []
    for t in range(dec_seq):
        acc = None
        for k in range(conv_w):
            term = xp[t + k] * cw[k:k + 1, :]
            acc = term if acc is None else acc + term
        xc.append(cb_ref[...] + acc)
    a, u = _lru_gates(jnp.concatenate(xc, axis=0), wra_ref[0], bra_ref[...], wri_ref[0], bri_ref[...],
                      lam_ref[...])
    h = h0_ref[...]
    for t in range(dec_seq):
        h = a[t * n_seq:(t + 1) * n_seq] * h + u[t * n_seq:(t + 1) * n_seq]
        o_ref[t] = (h * jax.nn.gelu(g_ref[t])).astype(o_ref.dtype)
    hl_ref[...] = h


def rnn_sample(x_t, st_t, g_t, h0, conv_w, conv_b, wra, bra, wri, bri, lam):
    dec_seq, n_seq, d_rnn = x_t.shape
    nb, cb = wra.shape[0], wra.shape[1]
    cwid = conv_w.shape[0]
    kern = functools.partial(_rnn_sample_kernel, conv_w=cwid)
    vec = lambda n: (0, n)
    blk = lambda n: (n, 0, 0)
    t3 = lambda n: (0, 0, n)
    return pl.pallas_call(
        kern,
        grid=(nb,),
        in_specs=[pl.BlockSpec((dec_seq, n_seq, cb), t3),
                  pl.BlockSpec((cwid - 1, n_seq, cb), t3),
                  pl.BlockSpec((dec_seq, n_seq, cb), t3),
                  pl.BlockSpec((n_seq, cb), vec),
                  pl.BlockSpec((cwid, cb), vec), pl.BlockSpec((1, cb), vec),
                  pl.BlockSpec((1, cb, cb), blk), pl.BlockSpec((1, cb), vec),
                  pl.BlockSpec((1, cb, cb), blk), pl.BlockSpec((1, cb), vec),
                  pl.BlockSpec((1, cb), vec)],
        out_specs=[pl.BlockSpec((dec_seq, n_seq, cb), t3),
                   pl.BlockSpec((n_seq, cb), vec)],
        out_shape=[jax.ShapeDtypeStruct((dec_seq, n_seq, d_rnn), BF16),
                   jax.ShapeDtypeStruct((n_seq, d_rnn), F32)],
        compiler_params=_params(("parallel",)),
        name="rnn_sample",
    )(x_t, st_t, g_t, h0, conv_w, conv_b.reshape(1, d_rnn), wra, bra.reshape(1, d_rnn),
      wri, bri.reshape(1, d_rnn), lam.reshape(1, d_rnn))


def _cast_pad_kernel(x_ref, o_ref, *, n_valid_blocks):
    j = pl.program_id(2)

    @pl.when(j < n_valid_blocks)
    def _():
        o_ref[...] = x_ref[...].astype(o_ref.dtype)

    @pl.when(j >= n_valid_blocks)
    def _():
        o_ref[...] = jnp.zeros_like(o_ref)


def cast_pad_cols(w, n_pad, tc=256):
    depth, k, n = w.shape
    assert n % tc == 0 and n_pad % tc == 0
    tr = _tile(k, 2048, 16)
    nvb = n // tc
    return pl.pallas_call(
        functools.partial(_cast_pad_kernel, n_valid_blocks=nvb),
        grid=(depth, k // tr, n_pad // tc),
        in_specs=[pl.BlockSpec((None, tr, tc), lambda l, i, j: (l, i, jnp.minimum(j, nvb - 1)))],
        out_specs=pl.BlockSpec((None, tr, tc), lambda l, i, j: (l, i, j)),
        out_shape=jax.ShapeDtypeStruct((depth, k, n_pad), BF16),
        compiler_params=_params(("parallel", "parallel", "arbitrary")),
        name="cast_pad_cols",
    )(w)


def _z_layout(q_rank, kv_rank, rope, d_rnn):
    off_kpe = q_rank + kv_rank
    off_xr = -(-(off_kpe + ROPE_GROUP) // d_rnn) * d_rnn
    off_gr = off_xr + d_rnn
    return dict(q_rank=q_rank, kv_rank=kv_rank, rope=rope, d_rnn=d_rnn, off_kpe=off_kpe,
                off_xr=off_xr, off_gr=off_gr, width=off_gr + d_rnn)


def _dup_rope_cols(w, rope):
    return jnp.concatenate([w, w], axis=-1)


def _prep_weights(lay, w_in, w_uq, w_uk, w_uv, w_ra, w_ri, w_o, w_gate, w_up, w_down, d_ff_pad):
    q_rank, kv_rank, rope = lay["q_rank"], lay["kv_rank"], lay["rope"]
    depth, d_model = w_in.shape[0], w_in.shape[1]
    n_heads, nope = w_uk.shape[2], w_uk.shape[3]
    o = q_rank + kv_rank
    pad = lay["off_xr"] - (lay["off_kpe"] + ROPE_GROUP)
    w_in = w_in.astype(BF16)
    w_in_p = jnp.concatenate(
        [w_in[:, :, :o], _dup_rope_cols(w_in[:, :, o:o + rope], rope), jnp.zeros((depth, d_model, pad), BF16),
         w_in[:, :, o + rope:]], axis=2)
    wq = w_uq.astype(BF16).reshape(depth, q_rank, n_heads, nope + rope)
    wq_nope = wq[..., :nope].reshape(depth, q_rank, n_heads * nope)
    wq_pe = _dup_rope_cols(wq[..., nope:], rope).reshape(depth, q_rank, n_heads * ROPE_GROUP)
    wuk_t = jnp.transpose(w_uk, (0, 2, 3, 1)).astype(BF16)
    wuv_h = jnp.transpose(w_uv, (0, 2, 1, 3)).astype(BF16)
    return dict(w_in=w_in_p, wq_nope=wq_nope, wq_pe=wq_pe, wuk_t=wuk_t, wuv=wuv_h,
                w_ra=w_ra.astype(BF16), w_ri=w_ri.astype(BF16), w_o=w_o.astype(BF16),
                wg=cast_pad_cols(w_gate, d_ff_pad), wu=cast_pad_cols(w_up, d_ff_pad), wd=w_down.astype(BF16))


def _rope_tables(pos, rope):
    half = rope // 2
    inv = ROPE_THETA ** (-jnp.arange(half, dtype=F32) / half)
    ang = pos.astype(F32)[:, None] * inv[None, :]
    c, s = jnp.cos(ang), jnp.sin(ang)
    zero = jnp.zeros_like(c)
    return (jnp.concatenate([c, c, zero, zero], axis=1),
            jnp.concatenate([-s, s, zero, zero], axis=1))


def kernel(x_prompt, x_sample, cache_ckv, cache_kpe, state_h, state_conv, page_table, norm_mix, w_in, q_norm, kv_norm, w_uq, w_uk, w_uv, conv_w, conv_b, w_ra, b_ra, w_ri, b_ri, lru_lambda, w_o, norm_ffn, w_gate, w_up, w_down, norm_final):
    batch, seq, d_model = x_prompt.shape
    n_seq, dec_seq, _ = x_sample.shape
    depth = norm_mix.shape[0]
    q_rank, kv_rank = q_norm.shape[1], kv_norm.shape[1]
    rope = cache_kpe.shape[3]
    n_heads, nope = w_uk.shape[2], w_uk.shape[3]
    d_rnn = state_h.shape[2]
    cwid = conv_w.shape[1]
    d_ff = w_gate.shape[2]
    past_len = page_table.shape[1] * cache_ckv.shape[2]
    assert 4 * (rope // 2) == ROPE_GROUP and dec_seq >= cwid - 1 and seq >= cwid - 1
    scale = float(nope + rope) ** -0.5
    lay = _z_layout(q_rank, kv_rank, rope, d_rnn)
    d_ff_pad = -(-d_ff // 1024) * 1024 if d_ff > 1024 else d_ff
    tq = 128

    n_p, n_s = batch * seq, n_seq * dec_seq
    nt = n_p + n_s
    assert n_p % tq == 0 and n_s % tq == 0 and tq % dec_seq == 0
    pos = jnp.concatenate([jnp.tile(jnp.arange(seq), batch), jnp.tile(past_len + jnp.arange(dec_seq), n_seq)])
    cs, sn = _rope_tables(pos, rope)
    cache_kpe_t = jnp.swapaxes(cache_kpe, 2, 3)

    x = jnp.concatenate([x_prompt.reshape(n_p, d_model), x_sample.reshape(n_s, d_model)], axis=0)
    outs = {k: [] for k in ("p_ckv", "p_kpe", "p_h", "p_conv", "s_ckv", "s_kpe", "s_h", "s_conv")}
    w = _prep_weights(lay, w_in, w_uq, w_uk, w_uv, w_ra, w_ri, w_o, w_gate, w_up, w_down, d_ff_pad)
    for l in range(depth):
        h = rmsnorm(x, norm_mix[l], BF16)
        z = matmul(h, w["w_in"], l, F32)
        qcat, ckv, kpe, kcat = mla_prep(z, lay, cs, sn, q_norm[l], kv_norm[l],
                                        w["wq_nope"], w["wq_pe"], w["wuk_t"], l, tq)
        dq = kcat.shape[1]
        o_attn_p = prompt_attention(qcat, kcat, w["wuv"], l, batch, seq, kv_rank, scale)
        seq_per_tile = tq // dec_seq
        q_s = qcat[n_p // tq:].reshape(n_s // tq, n_heads, seq_per_tile, dec_seq, dq)
        q_s = jnp.transpose(q_s, (0, 2, 1, 3, 4)).reshape(n_seq, n_heads * dec_seq, dq)
        k_new = kcat[n_p:].reshape(n_seq, dec_seq, dq)
        o_lat_s = sample_attention(q_s, k_new, cache_ckv, cache_kpe_t, page_table, l, scale)
        o_lat_s = jnp.transpose(o_lat_s.reshape(n_seq, n_heads, dec_seq, kv_rank), (1, 0, 2, 3))
        o_attn_s = uv_proj(o_lat_s.reshape(n_heads, n_s, kv_rank), w["wuv"], l)
        a_p, u_p = rnn_gates_prompt(z, lay, batch, seq, conv_w[l], conv_b[l], w["w_ra"][l], b_ra[l],
                                    w["w_ri"][l], b_ri[l], lru_lambda[l])
        o_rnn_p, h_last_p = rnn_scan_prompt(a_p, u_p, z, lay, batch, seq)
        xr_p = z[:n_p, lay["off_xr"]:lay["off_xr"] + d_rnn].reshape(batch, seq, d_rnn)
        z_s = z[n_p:].reshape(n_seq, dec_seq, -1)
        xr_s = z_s[:, :, lay["off_xr"]:lay["off_xr"] + d_rnn]
        gr_s = z_s[:, :, lay["off_gr"]:lay["off_gr"] + d_rnn]
        o_rnn_s, h_last_s = rnn_sample(jnp.swapaxes(xr_s, 0, 1), jnp.swapaxes(state_conv[l], 0, 1),
                                       jnp.swapaxes(gr_s, 0, 1), state_h[l], conv_w[l], conv_b[l],
                                       w["w_ra"][l], b_ra[l], w["w_ri"][l], b_ri[l], lru_lambda[l])
        o_rnn_s = jnp.swapaxes(o_rnn_s, 0, 1).reshape(n_s, d_rnn)
        x = matmul_res2(jnp.concatenate([o_attn_p, o_attn_s], axis=0),
                        jnp.concatenate([o_rnn_p, o_rnn_s], axis=0), w["w_o"], l, x)
        f = rmsnorm(x, norm_ffn[l], BF16)
        x = matmul_kres(swiglu(f, w["wg"], w["wu"], l), w["wd"], l, x)

        outs["p_ckv"].append(ckv[:n_p].reshape(batch, seq, kv_rank))
        outs["p_kpe"].append(kpe[:n_p].reshape(batch, seq, rope))
        outs["p_h"].append(h_last_p)
        outs["p_conv"].append(xr_p[:, seq - (cwid - 1):])
        outs["s_ckv"].append(ckv[n_p:].reshape(n_seq, dec_seq, kv_rank))
        outs["s_kpe"].append(kpe[n_p:].reshape(n_seq, dec_seq, rope))
        outs["s_h"].append(h_last_s)
        outs["s_conv"].append(xr_s[:, dec_seq - (cwid - 1):])

    y_prompt = rmsnorm(x, norm_final, F32, 0, n_p).reshape(batch, seq, d_model)
    y_sample = rmsnorm(x, norm_final, F32, n_p, n_s).reshape(n_seq, dec_seq, d_model)
    st = lambda k: jnp.stack(outs[k])
    return (y_prompt, y_sample, st("p_ckv"), st("p_kpe"), st("p_h"), st("p_conv"),
            st("s_ckv"), st("s_kpe"), st("s_h"), st("s_conv"))
```

```python
import functools

import jax
import jax.numpy as jnp
from jax import lax
from jax.experimental import pallas as pl
from jax.experimental.pallas import tpu as pltpu

EPS = 1e-6
ROPE_THETA = 10000.0
LRU_C = 8.0
LOG2E = 1.4426950408889634

LANES = 128
ROPE_GROUP = 128
VMEM_LIMIT = 56 * 1024 * 1024
MXU_GROUP_ROWS = 256

F32 = jnp.float32
BF16 = jnp.bfloat16


def _tile(n, pref, mult):
    best = None
    for t in range(mult, min(n, pref) + 1, mult):
        if n % t == 0:
            best = t
    return best if best is not None else n


def _params(sem):
    return pltpu.CompilerParams(dimension_semantics=sem, vmem_limit_bytes=VMEM_LIMIT)


def _dot(a, b):
    return jnp.dot(a, b, preferred_element_type=F32)


def _dot_nt(a, b):
    return lax.dot_general(a, b, (((1,), (1,)), ((), ())), preferred_element_type=F32)


def _rmsnorm_kernel(x_ref, g_ref, o_ref):
    x = x_ref[...]
    y = x * lax.rsqrt(jnp.mean(x * x, axis=-1, keepdims=True) + EPS)
    o_ref[...] = (y * g_ref[...]).astype(o_ref.dtype)


def rmsnorm(x, g, out_dtype, row0=0, rows=None):
    m, d = x.shape
    rows = m if rows is None else rows
    tm = _tile(rows, 512, 16)
    assert row0 % tm == 0
    off = row0 // tm
    return pl.pallas_call(
        _rmsnorm_kernel,
        grid=(rows // tm,),
        in_specs=[pl.BlockSpec((tm, d), lambda i: (i + off, 0)),
                  pl.BlockSpec((1, d), lambda i: (0, 0))],
        out_specs=pl.BlockSpec((tm, d), lambda i: (i, 0)),
        out_shape=jax.ShapeDtypeStruct((rows, d), out_dtype),
        compiler_params=_params(("parallel",)),
        name="rmsnorm",
    )(x, g.reshape(1, d))


def _mm_kernel(x_ref, w_ref, o_ref):
    o_ref[...] = _dot(x_ref[...], w_ref[...]).astype(o_ref.dtype)


def _mm_res2_kernel(xa_ref, xb_ref, wa_ref, wb_ref, r_ref, o_ref):
    o_ref[...] = r_ref[...] + (_dot(xa_ref[...], wa_ref[...]) + _dot(xb_ref[...], wb_ref[...]))


def _swiglu_kernel(x_ref, wg_ref, wu_ref, o_ref):
    x = x_ref[...]
    g = _dot(x, wg_ref[...])
    u = _dot(x, wu_ref[...])
    o_ref[...] = (jax.nn.silu(g) * u).astype(o_ref.dtype)


def _mm_kres_kernel(x_ref, w_ref, r_ref, o_ref, acc_ref):
    k = pl.program_id(2)

    @pl.when(k == 0)
    def _():
        acc_ref[...] = jnp.zeros_like(acc_ref)

    acc_ref[...] += _dot(x_ref[...], w_ref[...])

    @pl.when(k == pl.num_programs(2) - 1)
    def _():
        o_ref[...] = r_ref[...] + acc_ref[...]


def matmul(x, w, layer, out_dtype, tm_pref=1088, tn_pref=1024):
    m, k = x.shape
    n = w.shape[2]
    tm, tn = _tile(m, tm_pref, 16), _tile(n, tn_pref, LANES)
    return pl.pallas_call(
        _mm_kernel,
        grid=(m // tm, n // tn),
        in_specs=[pl.BlockSpec((tm, k), lambda i, j: (i, 0)),
                  pl.BlockSpec((None, k, tn), lambda i, j: (layer, 0, j))],
        out_specs=pl.BlockSpec((tm, tn), lambda i, j: (i, j)),
        out_shape=jax.ShapeDtypeStruct((m, n), out_dtype),
        compiler_params=_params(("parallel", "parallel")),
        name="matmul",
    )(x, w)


def matmul_res2(xa, xb, w, layer, res, tm_pref=1088, tn_pref=512):
    m, ka = xa.shape
    kb = xb.shape[1]
    n = w.shape[2]
    assert ka == kb and w.shape[1] == ka + kb
    tm, tn = _tile(m, tm_pref, 16), _tile(n, tn_pref, LANES)
    return pl.pallas_call(
        _mm_res2_kernel,
        grid=(m // tm, n // tn),
        in_specs=[pl.BlockSpec((tm, ka), lambda i, j: (i, 0)),
                  pl.BlockSpec((tm, kb), lambda i, j: (i, 0)),
                  pl.BlockSpec((None, ka, tn), lambda i, j: (layer, 0, j)),
                  pl.BlockSpec((None, kb, tn), lambda i, j: (layer, 1, j)),
                  pl.BlockSpec((tm, tn), lambda i, j: (i, j))],
        out_specs=pl.BlockSpec((tm, tn), lambda i, j: (i, j)),
        out_shape=jax.ShapeDtypeStruct((m, n), F32),
        compiler_params=_params(("parallel", "parallel")),
        name="matmul_res2",
    )(xa, xb, w, w, res)


def swiglu(x, wg, wu, layer, tm_pref=1088, tn_pref=512):
    m, k = x.shape
    n = wg.shape[2]
    tm, tn = _tile(m, tm_pref, 16), _tile(n, tn_pref, LANES)
    w_spec = pl.BlockSpec((None, k, tn), lambda i, j: (layer, 0, j))
    return pl.pallas_call(
        _swiglu_kernel,
        grid=(m // tm, n // tn),
        in_specs=[pl.BlockSpec((tm, k), lambda i, j: (i, 0)), w_spec, w_spec],
        out_specs=pl.BlockSpec((tm, tn), lambda i, j: (i, j)),
        out_shape=jax.ShapeDtypeStruct((m, n), BF16),
        compiler_params=_params(("parallel", "parallel")),
        name="swiglu",
    )(x, wg, wu)


def matmul_kres(x, w, layer, res, tm_pref=1088, tn_pref=512, tk_pref=5504):
    m = x.shape[0]
    k, n = w.shape[1], w.shape[2]
    assert x.shape[1] >= k
    tm, tn, tk = _tile(m, tm_pref, 16), _tile(n, tn_pref, LANES), _tile(k, tk_pref, LANES)
    return pl.pallas_call(
        _mm_kres_kernel,
        grid=(m // tm, n // tn, k // tk),
        in_specs=[pl.BlockSpec((tm, tk), lambda i, j, kk: (i, kk)),
                  pl.BlockSpec((None, tk, tn), lambda i, j, kk: (layer, kk, j)),
                  pl.BlockSpec((tm, tn), lambda i, j, kk: (i, j))],
        out_specs=pl.BlockSpec((tm, tn), lambda i, j, kk: (i, j)),
        out_shape=jax.ShapeDtypeStruct((m, n), F32),
        scratch_shapes=[pltpu.VMEM((tm, tn), F32)],
        compiler_params=_params(("parallel", "parallel", "arbitrary")),
        name="matmul_kres",
    )(x, w, res)


def _rope_group(x, cs, sn):
    return x * cs + pltpu.roll(x, ROPE_GROUP // 4, 1) * sn


def _mla_prep_kernel(qc_ref, kvc_ref, kpe_ref, cs_ref, sn_ref, qg_ref, kvg_ref,
                     wqn_ref, wqp_ref, wuk_ref,
                     qcat_ref, ckv_ref, kpeo_ref, kcat_ref, *, n_heads, nope, kv_rank, rope):
    tq = qcat_ref.shape[2]
    n_sub = qcat_ref.shape[0]
    cs = cs_ref[...]
    sn = sn_ref[...]

    qc = qc_ref[...]
    qn = (qc * lax.rsqrt(jnp.mean(qc * qc, axis=-1, keepdims=True) + EPS) * qg_ref[...]).astype(BF16)
    q_nope = _dot(qn, wqn_ref[...])
    q_pe = _dot(qn, wqp_ref[...])
    for h in range(n_heads):
        ql = _dot(q_nope[:, h * nope:(h + 1) * nope].astype(BF16), wuk_ref[h]).astype(BF16)
        rp = _rope_group(q_pe[:, h * ROPE_GROUP:(h + 1) * ROPE_GROUP], cs, sn).astype(BF16)
        for t in range(n_sub):
            qcat_ref[t, h, :, 0:kv_rank] = ql[t * tq:(t + 1) * tq]
            qcat_ref[t, h, :, kv_rank:kv_rank + ROPE_GROUP] = rp[t * tq:(t + 1) * tq]

    kvc = kvc_ref[...]
    ckv = kvc * lax.rsqrt(jnp.mean(kvc * kvc, axis=-1, keepdims=True) + EPS) * kvg_ref[...]
    kr = _rope_group(kpe_ref[...], cs, sn)
    ckv_ref[...] = ckv
    kpeo_ref[...] = kr[:, 0:rope]
    kcat_ref[:, 0:kv_rank] = ckv.astype(BF16)
    kcat_ref[:, kv_rank:kv_rank + ROPE_GROUP] = kr.astype(BF16)


def mla_prep(z, lay, cs, sn, q_norm, kv_norm, wq_nope, wq_pe, wuk_t, layer, tq):
    nt = z.shape[0]
    qr, kv, rope = lay["q_rank"], lay["kv_rank"], lay["rope"]
    n_heads, nope = wuk_t.shape[1], wuk_t.shape[2]
    dq = kv + ROPE_GROUP
    tm = 2 * tq if nt % (2 * tq) == 0 else tq
    n_sub = tm // tq
    assert nt % tm == 0 and qr % kv == 0 and lay["off_kpe"] % ROPE_GROUP == 0
    kern = functools.partial(_mla_prep_kernel, n_heads=n_heads, nope=nope, kv_rank=kv, rope=rope)
    const2 = lambda i: (0, 0)
    return pl.pallas_call(
        kern,
        grid=(nt // tm,),
        in_specs=[pl.BlockSpec((tm, qr), lambda i: (i, 0)),
                  pl.BlockSpec((tm, kv), lambda i: (i, qr // kv)),
                  pl.BlockSpec((tm, ROPE_GROUP), lambda i: (i, lay["off_kpe"] // ROPE_GROUP)),
                  pl.BlockSpec((tm, ROPE_GROUP), lambda i: (i, 0)),
                  pl.BlockSpec((tm, ROPE_GROUP), lambda i: (i, 0)),
                  pl.BlockSpec((1, qr), const2),
                  pl.BlockSpec((1, kv), const2),
                  pl.BlockSpec((None,) + wq_nope.shape[1:], lambda i: (layer, 0, 0)),
                  pl.BlockSpec((None,) + wq_pe.shape[1:], lambda i: (layer, 0, 0)),
                  pl.BlockSpec((None,) + wuk_t.shape[1:], lambda i: (layer, 0, 0, 0))],
        out_specs=[pl.BlockSpec((n_sub, n_heads, tq, dq), lambda i: (i, 0, 0, 0)),
                   pl.BlockSpec((tm, kv), lambda i: (i, 0)),
                   pl.BlockSpec((tm, rope), lambda i: (i, 0)),
                   pl.BlockSpec((tm, dq), lambda i: (i, 0))],
        out_shape=[jax.ShapeDtypeStruct((nt // tq, n_heads, tq, dq), BF16),
                   jax.ShapeDtypeStruct((nt, kv), F32),
                   jax.ShapeDtypeStruct((nt, rope), F32),
                   jax.ShapeDtypeStruct((nt, dq), BF16)],
        compiler_params=_params(("parallel",)),
        name="mla_prep",
    )(z, z, z, cs, sn, q_norm.reshape(1, qr), kv_norm.reshape(1, kv), wq_nope, wq_pe, wuk_t)


def _prompt_attn_kernel(q_ref, k_ref, wuv_ref, o_ref, m_ref, l_ref, acc_ref, s_ref, p_ref, alpha_ref,
                        *, tq, tk, kv_rank, scale, chunk_rows, n_groups):
    i = pl.program_id(1)
    j = pl.program_id(2)
    n_heads = q_ref.shape[1]
    rows = n_heads * tq

    @pl.when(j == 0)
    def _():
        m_ref[...] = jnp.full_like(m_ref, -jnp.inf)
        l_ref[...] = jnp.zeros_like(l_ref)
        acc_ref[...] = jnp.zeros_like(acc_ref)

    last = ((i + 1) * tq - 1) // tk
    c2 = scale * LOG2E

    def block(masked):
        k = k_ref[...]
        v = k[:, 0:kv_rank]
        g_rows = rows // n_groups
        hpg = n_heads // n_groups
        for g in range(n_groups):
            gs = slice(g * g_rows, (g + 1) * g_rows)
            s_ref[gs, :] = _dot_nt(q_ref[0, g * hpg:(g + 1) * hpg].reshape(g_rows, q_ref.shape[3]), k)
        for g in range(n_groups):
            gs = slice(g * g_rows, (g + 1) * g_rows)
            for c in range(g_rows // chunk_rows):
                r0 = g * g_rows + c * chunk_rows
                rs = slice(r0, r0 + chunk_rows)
                s = s_ref[rs, :]
                if masked:
                    row = r0 + lax.broadcasted_iota(jnp.int32, (chunk_rows, 1), 0)
                    q_pos = i * tq + (row & (tq - 1))
                    k_pos = j * tk + lax.broadcasted_iota(jnp.int32, (1, tk), 1)
                    s = jnp.where(k_pos <= q_pos, s, -jnp.inf)
                m_prev = m_ref[rs, :]
                m_new = jnp.maximum(m_prev, jnp.max(s, axis=-1, keepdims=True))
                alpha = jnp.exp2((m_prev - m_new) * c2)
                p = jnp.exp2((s - m_new) * c2)
                l_ref[rs, :] = alpha * l_ref[rs, :] + jnp.sum(p, axis=-1, keepdims=True)
                m_ref[rs, :] = m_new
                alpha_ref[rs, :] = alpha
                p_ref[rs, :] = p.astype(BF16)
            acc_ref[gs, :] = alpha_ref[gs, :] * acc_ref[gs, :] + _dot(p_ref[gs, :], v)

    pl.when(j < last)(functools.partial(block, False))
    pl.when(j == last)(functools.partial(block, True))

    @pl.when(j == pl.num_programs(2) - 1)
    def _():
        vh = wuv_ref.shape[2]
        for h in range(n_heads):
            sl = slice(h * tq, (h + 1) * tq)
            o_lat = (acc_ref[sl, :] / l_ref[sl, :]).astype(BF16)
            o_ref[:, h * vh:(h + 1) * vh] = _dot(o_lat, wuv_ref[h]).astype(o_ref.dtype)


def prompt_attention(qcat, kcat, wuv, layer, batch, seq, kv_rank, scale):
    n_heads, tq, dq = qcat.shape[1], qcat.shape[2], qcat.shape[3]
    assert tq & (tq - 1) == 0 and seq % tq == 0
    tk = _tile(seq, 512, tq)
    nq, nk = seq // tq, seq // tk
    vh = wuv.shape[3]
    rows = n_heads * tq
    chunk_rows = 128
    kern = functools.partial(_prompt_attn_kernel, tq=tq, tk=tk, kv_rank=kv_rank, scale=scale,
                             chunk_rows=chunk_rows, n_groups=max(1, rows // MXU_GROUP_ROWS))

    def k_map(b, i, j):
        return (b * nk + jnp.minimum(j, ((i + 1) * tq - 1) // tk), 0)

    return pl.pallas_call(
        kern,
        grid=(batch, nq, nk),
        in_specs=[pl.BlockSpec((1, n_heads, tq, dq), lambda b, i, j: (b * nq + i, 0, 0, 0)),
                  pl.BlockSpec((tk, dq), k_map),
                  pl.BlockSpec((None,) + wuv.shape[1:], lambda b, i, j: (layer, 0, 0, 0))],
        out_specs=pl.BlockSpec((tq, n_heads * vh), lambda b, i, j: (b * nq + i, 0)),
        out_shape=jax.ShapeDtypeStruct((batch * seq, n_heads * vh), BF16),
        scratch_shapes=[pltpu.VMEM((rows, 1), F32), pltpu.VMEM((rows, 1), F32),
                        pltpu.VMEM((rows, kv_rank), F32),
                        pltpu.VMEM((rows, tk), F32), pltpu.VMEM((rows, tk), BF16), pltpu.VMEM((rows, 1), F32)],
        compiler_params=_params(("parallel", "parallel", "arbitrary")),
        name="prompt_attention",
    )(qcat, kcat, wuv)


def _sample_attn_kernel(pt_ref, q_ref, kn_ref, ckv_hbm, kpe_hbm, o_ref,
                        ckv_in, kpe_in, kc_buf, kp_buf, m_ref, l_ref, acc_ref, sem,
                        *, layer, n_pages_step, page, kv_rank, rope, dec_seq, scale):
    b = pl.program_id(0)
    c = pl.program_id(1)
    n_chunks = pl.num_programs(1)
    step = b * n_chunks + c
    slot = step % 2

    def page_copies(bb, cc, sl):
        out = []
        for p_i in range(n_pages_step):
            pg = pt_ref[bb, cc * n_pages_step + p_i]
            out.append(pltpu.make_async_copy(ckv_hbm.at[layer, pg], ckv_in.at[sl, p_i], sem.at[sl]))
            out.append(pltpu.make_async_copy(kpe_hbm.at[layer, pg], kpe_in.at[sl, p_i], sem.at[sl]))
        return out

    def start_all(copies):
        for n, cp in enumerate(copies):
            cp.start(priority=(n // 2) % 2)

    @pl.when(step == 0)
    def _():
        start_all(page_copies(b, c, slot))

    @pl.when(step + 1 < pl.num_programs(0) * n_chunks)
    def _():
        wrap = c + 1 == n_chunks
        start_all(page_copies(jnp.where(wrap, b + 1, b), jnp.where(wrap, 0, c + 1), 1 - slot))

    @pl.when(c == 0)
    def _():
        m_ref[...] = jnp.full_like(m_ref, -jnp.inf)
        l_ref[...] = jnp.zeros_like(l_ref)
        acc_ref[...] = jnp.zeros_like(acc_ref)
        kp_buf[rope:ROPE_GROUP, :] = jnp.zeros((ROPE_GROUP - rope, kp_buf.shape[1]), BF16)

    q = q_ref[0]
    q_lat = q[:, 0:kv_rank]
    q_pe = q[:, kv_rank:kv_rank + ROPE_GROUP]

    def update(s, v):
        m_prev = m_ref[...]
        m_new = jnp.maximum(m_prev, jnp.max(s, axis=-1, keepdims=True))
        alpha = jnp.exp(m_prev - m_new)
        p = jnp.exp(s - m_new)
        l_ref[...] = alpha * l_ref[...] + jnp.sum(p, axis=-1, keepdims=True)
        acc_ref[...] = alpha * acc_ref[...] + _dot(p.astype(BF16), v)
        m_ref[...] = m_new

    for cp in page_copies(b, c, slot):
        cp.wait()
    for p_i in range(n_pages_step):
        kc_buf[p_i * page:(p_i + 1) * page, :] = ckv_in[slot, p_i].astype(BF16)
        kp_buf[0:rope, p_i * page:(p_i + 1) * page] = kpe_in[slot, p_i].astype(BF16)
    kc = kc_buf[...]
    s = (_dot_nt(q_lat, kc) + _dot(q_pe, kp_buf[...])) * scale
    update(s, kc)

    @pl.when(c == pl.num_programs(1) - 1)
    def _():
        kn = kn_ref[0]
        kn_lat = kn[:, 0:kv_rank]
        s_new = (_dot_nt(q_lat, kn_lat) + _dot_nt(q_pe, kn[:, kv_rank:kv_rank + ROPE_GROUP])) * scale
        rows = q.shape[0]
        q_t = lax.broadcasted_iota(jnp.int32, (rows, 1), 0) % dec_seq
        k_t = lax.broadcasted_iota(jnp.int32, (1, kn.shape[0]), 1)
        update(jnp.where(k_t <= q_t, s_new, -jnp.inf), kn_lat)
        o_ref[0] = (acc_ref[...] / l_ref[...]).astype(o_ref.dtype)


def sample_attention(q_s, k_new, cache_ckv, cache_kpe_t, page_table, layer, scale, pages_per_step=16):
    n_seq, rows, dq = q_s.shape
    dec_seq = k_new.shape[1]
    k_new = jnp.pad(k_new, ((0, 0), (0, LANES - dec_seq), (0, 0)))
    page, kv_rank = cache_ckv.shape[2], cache_ckv.shape[3]
    rope = cache_kpe_t.shape[2]
    n_pages = page_table.shape[1]
    pps = _tile(n_pages, pages_per_step, 1)
    kern = functools.partial(_sample_attn_kernel, layer=layer, n_pages_step=pps, page=page, kv_rank=kv_rank,
                             rope=rope, dec_seq=dec_seq, scale=scale)
    grid_spec = pltpu.PrefetchScalarGridSpec(
        num_scalar_prefetch=1,
        grid=(n_seq, n_pages // pps),
        in_specs=[pl.BlockSpec((1, rows, dq), lambda b, c, pt: (b, 0, 0)),
                  pl.BlockSpec((1, LANES, dq), lambda b, c, pt: (b, 0, 0)),
                  pl.BlockSpec(memory_space=pl.ANY),
                  pl.BlockSpec(memory_space=pl.ANY)],
        out_specs=pl.BlockSpec((1, rows, kv_rank), lambda b, c, pt: (b, 0, 0)),
        scratch_shapes=[pltpu.VMEM((2, pps, page, kv_rank), F32),
                        pltpu.VMEM((2, pps, rope, page), F32),
                        pltpu.VMEM((pps * page, kv_rank), BF16),
                        pltpu.VMEM((ROPE_GROUP, pps * page), BF16),
                        pltpu.VMEM((rows, 1), F32), pltpu.VMEM((rows, 1), F32),
                        pltpu.VMEM((rows, kv_rank), F32),
                        pltpu.SemaphoreType.DMA((2,))])
    return pl.pallas_call(
        kern,
        grid_spec=grid_spec,
        out_shape=jax.ShapeDtypeStruct((n_seq, rows, kv_rank), BF16),
        compiler_params=_params(("arbitrary", "arbitrary")),
        name="sample_attention",
    )(page_table, q_s, k_new, cache_ckv, cache_kpe_t)


def _uv_proj_kernel(o_ref, w_ref, out_ref):
    out_ref[...] = _dot(o_ref[0], w_ref[0]).astype(out_ref.dtype)


def uv_proj(o_lat_hm, wuv, layer):
    n_heads, n, kv_rank = o_lat_hm.shape
    vh = wuv.shape[3]
    return pl.pallas_call(
        _uv_proj_kernel,
        grid=(n_heads,),
        in_specs=[pl.BlockSpec((1, n, kv_rank), lambda h: (h, 0, 0)),
                  pl.BlockSpec((None, 1, kv_rank, vh), lambda h: (layer, h, 0, 0))],
        out_specs=pl.BlockSpec((n, vh), lambda h: (0, h)),
        out_shape=jax.ShapeDtypeStruct((n, n_heads * vh), BF16),
        compiler_params=_params(("parallel",)),
        name="uv_proj",
    )(o_lat_hm, wuv)


def _lru_gates(xc, wra, bra, wri, bri, lam):
    xb = xc.astype(BF16)
    r = jax.nn.sigmoid(_dot(xb, wra) + bra)
    i = jax.nn.sigmoid(_dot(xb, wri) + bri)
    log_a = -LRU_C * r * jax.nn.softplus(-lam)
    a = jnp.exp(log_a)
    one_minus_a2 = -jnp.tanh(log_a) * (a * a + 1.0)
    u = jnp.sqrt(one_minus_a2) * (i * xc)
    return a, u


def _rnn_gates_prompt_kernel(x_ref, cw_ref, cb_ref, wra_ref, bra_ref, wri_ref, bri_ref, lam_ref,
                             a_ref, u_ref, *, conv_w):
    x = x_ref[...]
    t_idx = lax.broadcasted_iota(jnp.int32, (x.shape[0], 1), 0)
    cw = cw_ref[...]
    acc = None
    for k in range(conv_w):
        d = conv_w - 1 - k
        xs = x if d == 0 else jnp.where(t_idx >= d, pltpu.roll(x, d, 0), 0.0)
        term = xs * cw[k:k + 1, :]
        acc = term if acc is None else acc + term
    xc = cb_ref[...] + acc
    a, u = _lru_gates(xc, wra_ref[0], bra_ref[...], wri_ref[0], bri_ref[...], lam_ref[...])
    a_ref[...] = a
    u_ref[...] = u


def rnn_gates_prompt(z, lay, batch, seq, conv_w, conv_b, wra, bra, wri, bri, lam):
    d_rnn = lay["d_rnn"]
    nb, cb = wra.shape[0], wra.shape[1]
    assert cb == LANES
    x_off = lay["off_xr"] // cb
    kern = functools.partial(_rnn_gates_prompt_kernel, conv_w=conv_w.shape[0])
    vec = lambda b, n: (0, n)
    blk = lambda b, n: (n, 0, 0)
    a, u = pl.pallas_call(
        kern,
        grid=(batch, nb),
        in_specs=[pl.BlockSpec((seq, cb), lambda b, n: (b, x_off + n)),
                  pl.BlockSpec((conv_w.shape[0], cb), vec),
                  pl.BlockSpec((1, cb), vec),
                  pl.BlockSpec((1, cb, cb), blk), pl.BlockSpec((1, cb), vec),
                  pl.BlockSpec((1, cb, cb), blk), pl.BlockSpec((1, cb), vec),
                  pl.BlockSpec((1, cb), vec)],
        out_specs=[pl.BlockSpec((seq, cb), lambda b, n: (b, n))] * 2,
        out_shape=[jax.ShapeDtypeStruct((batch * seq, d_rnn), F32)] * 2,
        compiler_params=_params(("parallel", "parallel")),
        name="rnn_gates_prompt",
    )(z, conv_w, conv_b.reshape(1, d_rnn), wra, bra.reshape(1, d_rnn), wri, bri.reshape(1, d_rnn),
      lam.reshape(1, d_rnn))
    return a, u


def _rnn_scan_prompt_kernel(a_ref, u_ref, g_ref, o_ref, hl_ref, h_ref, hs_ref):
    tt = pl.program_id(1)

    @pl.when(tt == 0)
    def _():
        h_ref[...] = jnp.zeros_like(h_ref)

    def step(t, h):
        h = a_ref[t] * h + u_ref[t]
        hs_ref[t] = h
        return h

    h = lax.fori_loop(0, a_ref.shape[0], step, h_ref[...], unroll=8)
    h_ref[...] = h
    o_ref[...] = (hs_ref[...] * jax.nn.gelu(g_ref[...])).astype(o_ref.dtype)

    @pl.when(tt == pl.num_programs(1) - 1)
    def _():
        hl_ref[0] = h


def rnn_scan_prompt(a, u, z, lay, batch, seq):
    d_rnn = lay["d_rnn"]
    nb = d_rnn // LANES
    assert lay["off_gr"] % d_rnn == 0 and z.shape[1] % LANES == 0
    tc = _tile(seq, 512, 8)
    nt = seq // tc
    a4 = a.reshape(batch * seq, nb, LANES)
    u4 = u.reshape(batch * seq, nb, LANES)
    z4 = z.reshape(z.shape[0], z.shape[1] // LANES, LANES)
    g_blk = lay["off_gr"] // d_rnn
    row = lambda b, t: (b * nt + t, 0, 0)
    o, hl = pl.pallas_call(
        _rnn_scan_prompt_kernel,
        grid=(batch, nt),
        in_specs=[pl.BlockSpec((tc, nb, LANES), row),
                  pl.BlockSpec((tc, nb, LANES), row),
                  pl.BlockSpec((tc, nb, LANES), lambda b, t: (b * nt + t, g_blk, 0))],
        out_specs=[pl.BlockSpec((tc, nb, LANES), row),
                   pl.BlockSpec((1, nb, LANES), lambda b, t: (b, 0, 0))],
        out_shape=[jax.ShapeDtypeStruct((batch * seq, nb, LANES), BF16),
                   jax.ShapeDtypeStruct((batch, nb, LANES), F32)],
        scratch_shapes=[pltpu.VMEM((nb, LANES), F32), pltpu.VMEM((tc, nb, LANES), F32)],
        compiler_params=_params(("parallel", "arbitrary")),
        name="rnn_scan_prompt",
    )(a4, u4, z4)
    return o.reshape(batch * seq, d_rnn), hl.reshape(batch, d_rnn)


def _rnn_sample_kernel(x_ref, st_ref, g_ref, h0_ref, cw_ref, cb_ref, wra_ref, bra_ref, wri_ref, bri_ref,
                       lam_ref, o_ref, hl_ref, *, conv_w):
    dec_seq, n_seq = x_ref.shape[0], x_ref.shape[1]
    cw = cw_ref[...]
    xp = [st_ref[k] for k in range(conv_w - 1)] + [x_ref[t] for t in range(dec_seq)]
    xc = []
    for t in range(dec_seq):
        acc = None
        for k in range(conv_w):
            term = xp[t + k] * cw[k:k + 1, :]
            acc = term if acc is None else acc + term
        xc.append(cb_ref[...] + acc)
    a, u = _lru_gates(jnp.concatenate(xc, axis=0), wra_ref[0], bra_ref[...], wri_ref[0], bri_ref[...],
                      lam_ref[...])
    h = h0_ref[...]
    for t in range(dec_seq):
        h = a[t * n_seq:(t + 1) * n_seq] * h + u[t * n_seq:(t + 1) * n_seq]
        o_ref[t] = (h * jax.nn.gelu(g_ref[t])).astype(o_ref.dtype)
    hl_ref[...] = h


def rnn_sample(x_t, st_t, g_t, h0, conv_w, conv_b, wra, bra, wri, bri, lam):
    dec_seq, n_seq, d_rnn = x_t.shape
    nb, cb = wra.shape[0], wra.shape[1]
    cwid = conv_w.shape[0]
    kern = functools.partial(_rnn_sample_kernel, conv_w=cwid)
    vec = lambda n: (0, n)
    blk = lambda n: (n, 0, 0)
    t3 = lambda n: (0, 0, n)
    return pl.pallas_call(
        kern,
        grid=(nb,),
        in_specs=[pl.BlockSpec((dec_seq, n_seq, cb), t3),
                  pl.BlockSpec((cwid - 1, n_seq, cb), t3),
                  pl.BlockSpec((dec_seq, n_seq, cb), t3),
                  pl.BlockSpec((n_seq, cb), vec),
                  pl.BlockSpec((cwid, cb), vec), pl.BlockSpec((1, cb), vec),
                  pl.BlockSpec((1, cb, cb), blk), pl.BlockSpec((1, cb), vec),
                  pl.BlockSpec((1, cb, cb), blk), pl.BlockSpec((1, cb), vec),
                  pl.BlockSpec((1, cb), vec)],
        out_specs=[pl.BlockSpec((dec_seq, n_seq, cb), t3),
                   pl.BlockSpec((n_seq, cb), vec)],
        out_shape=[jax.ShapeDtypeStruct((dec_seq, n_seq, d_rnn), BF16),
                   jax.ShapeDtypeStruct((n_seq, d_rnn), F32)],
        compiler_params=_params(("parallel",)),
        name="rnn_sample",
    )(x_t, st_t, g_t, h0, conv_w, conv_b.reshape(1, d_rnn), wra, bra.reshape(1, d_rnn),
      wri, bri.reshape(1, d_rnn), lam.reshape(1, d_rnn))


def _cast_pad_kernel(x_ref, o_ref, *, n_valid_blocks):
    j = pl.program_id(2)

    @pl.when(j < n_valid_blocks)
    def _():
        o_ref[...] = x_ref[...].astype(o_ref.dtype)

    @pl.when(j >= n_valid_blocks)
    def _():
        o_ref[...] = jnp.zeros_like(o_ref)


def cast_pad_cols(w, n_pad, tc=256):
    depth, k, n = w.shape
    assert n % tc == 0 and n_pad % tc == 0
    tr = _tile(k, 2048, 16)
    nvb = n // tc
    return pl.pallas_call(
        functools.partial(_cast_pad_kernel, n_valid_blocks=nvb),
        grid=(depth, k // tr, n_pad // tc),
        in_specs=[pl.BlockSpec((None, tr, tc), lambda l, i, j: (l, i, jnp.minimum(j, nvb - 1)))],
        out_specs=pl.BlockSpec((None, tr, tc), lambda l, i, j: (l, i, j)),
        out_shape=jax.ShapeDtypeStruct((depth, k, n_pad), BF16),
        compiler_params=_params(("parallel", "parallel", "arbitrary")),
        name="cast_pad_cols",
    )(w)


def _z_layout(q_rank, kv_rank, rope, d_rnn):
    off_kpe = q_rank + kv_rank
    off_xr = -(-(off_kpe + ROPE_GROUP) // d_rnn) * d_rnn
    off_gr = off_xr + d_rnn
    return dict(q_rank=q_rank, kv_rank=kv_rank, rope=rope, d_rnn=d_rnn, off_kpe=off_kpe,
                off_xr=off_xr, off_gr=off_gr, width=off_gr + d_rnn)


def _dup_rope_cols(w, rope):
    return jnp.concatenate([w, w], axis=-1)


def _prep_weights(lay, w_in, w_uq, w_uk, w_uv, w_ra, w_ri, w_o, w_gate, w_up, w_down, d_ff_pad):
    q_rank, kv_rank, rope = lay["q_rank"], lay["kv_rank"], lay["rope"]
    depth, d_model = w_in.shape[0], w_in.shape[1]
    n_heads, nope = w_uk.shape[2], w_uk.shape[3]
    o = q_rank + kv_rank
    pad = lay["off_xr"] - (lay["off_kpe"] + ROPE_GROUP)
    w_in = w_in.astype(BF16)
    w_in_p = jnp.concatenate(
        [w_in[:, :, :o], _dup_rope_cols(w_in[:, :, o:o + rope], rope), jnp.zeros((depth, d_model, pad), BF16),
         w_in[:, :, o + rope:]], axis=2)
    wq = w_uq.astype(BF16).reshape(depth, q_rank, n_heads, nope + rope)
    wq_nope = wq[..., :nope].reshape(depth, q_rank, n_heads * nope)
    wq_pe = _dup_rope_cols(wq[..., nope:], rope).reshape(depth, q_rank, n_heads * ROPE_GROUP)
    wuk_t = jnp.transpose(w_uk, (0, 2, 3, 1)).astype(BF16)
    wuv_h = jnp.transpose(w_uv, (0, 2, 1, 3)).astype(BF16)
    return dict(w_in=w_in_p, wq_nope=wq_nope, wq_pe=wq_pe, wuk_t=wuk_t, wuv=wuv_h,
                w_ra=w_ra.astype(BF16), w_ri=w_ri.astype(BF16), w_o=w_o.astype(BF16),
                wg=cast_pad_cols(w_gate, d_ff_pad), wu=cast_pad_cols(w_up, d_ff_pad), wd=w_down.astype(BF16))


def _rope_tables(pos, rope):
    half = rope // 2
    inv = ROPE_THETA ** (-jnp.arange(half, dtype=F32) / half)
    ang = pos.astype(F32)[:, None] * inv[None, :]
    c, s = jnp.cos(ang), jnp.sin(ang)
    zero = jnp.zeros_like(c)
    return (jnp.concatenate([c, c, zero, zero], axis=1),
            jnp.concatenate([-s, s, zero, zero], axis=1))


def kernel(x_prompt, x_sample, cache_ckv, cache_kpe, state_h, state_conv, page_table, norm_mix, w_in, q_norm, kv_norm, w_uq, w_uk, w_uv, conv_w, conv_b, w_ra, b_ra, w_ri, b_ri, lru_lambda, w_o, norm_ffn, w_gate, w_up, w_down, norm_final):
    batch, seq, d_model = x_prompt.shape
    n_seq, dec_seq, _ = x_sample.shape
    depth = norm_mix.shape[0]
    q_rank, kv_rank = q_norm.shape[1], kv_norm.shape[1]
    rope = cache_kpe.shape[3]
    n_heads, nope = w_uk.shape[2], w_uk.shape[3]
    d_rnn = state_h.shape[2]
    cwid = conv_w.shape[1]
    d_ff = w_gate.shape[2]
    past_len = page_table.shape[1] * cache_ckv.shape[2]
    assert 4 * (rope // 2) == ROPE_GROUP and dec_seq >= cwid - 1 and seq >= cwid - 1
    scale = float(nope + rope) ** -0.5
    lay = _z_layout(q_rank, kv_rank, rope, d_rnn)
    d_ff_pad = -(-d_ff // 1024) * 1024 if d_ff > 1024 else d_ff
    tq = 128

    n_p, n_s = batch * seq, n_seq * dec_seq
    nt = n_p + n_s
    assert n_p % tq == 0 and n_s % tq == 0 and tq % dec_seq == 0
    pos = jnp.concatenate([jnp.tile(jnp.arange(seq), batch), jnp.tile(past_len + jnp.arange(dec_seq), n_seq)])
    cs, sn = _rope_tables(pos, rope)
    cache_kpe_t = jnp.swapaxes(cache_kpe, 2, 3)

    x = jnp.concatenate([x_prompt.reshape(n_p, d_model), x_sample.reshape(n_s, d_model)], axis=0)
    outs = {k: [] for k in ("p_ckv", "p_kpe", "p_h", "p_conv", "s_ckv", "s_kpe", "s_h", "s_conv")}
    w = _prep_weights(lay, w_in, w_uq, w_uk, w_uv, w_ra, w_ri, w_o, w_gate, w_up, w_down, d_ff_pad)
    for l in range(depth):
        h = rmsnorm(x, norm_mix[l], BF16)
        z = matmul(h, w["w_in"], l, F32)
        qcat, ckv, kpe, kcat = mla_prep(z, lay, cs, sn, q_norm[l], kv_norm[l],
                                        w["wq_nope"], w["wq_pe"], w["wuk_t"], l, tq)
        dq = kcat.shape[1]
        o_attn_p = prompt_attention(qcat, kcat, w["wuv"], l, batch, seq, kv_rank, scale)
        seq_per_tile = tq // dec_seq
        q_s = qcat[n_p // tq:].reshape(n_s // tq, n_heads, seq_per_tile, dec_seq, dq)
        q_s = jnp.transpose(q_s, (0, 2, 1, 3, 4)).reshape(n_seq, n_heads * dec_seq, dq)
        k_new = kcat[n_p:].reshape(n_seq, dec_seq, dq)
        o_lat_s = sample_attention(q_s, k_new, cache_ckv, cache_kpe_t, page_table, l, scale)
        o_lat_s = jnp.transpose(o_lat_s.reshape(n_seq, n_heads, dec_seq, kv_rank), (1, 0, 2, 3))
        o_attn_s = uv_proj(o_lat_s.reshape(n_heads, n_s, kv_rank), w["wuv"], l)
        a_p, u_p = rnn_gates_prompt(z, lay, batch, seq, conv_w[l], conv_b[l], w["w_ra"][l], b_ra[l],
                                    w["w_ri"][l], b_ri[l], lru_lambda[l])
        o_rnn_p, h_last_p = rnn_scan_prompt(a_p, u_p, z, lay, batch, seq)
        conv_p = jnp.stack([z[(b + 1) * seq - (cwid - 1):(b + 1) * seq, lay["off_xr"]:lay["off_xr"] + d_rnn]
                            for b in range(batch)])
        z_s = z[n_p:].reshape(n_seq, dec_seq, -1)
        xr_s = z_s[:, :, lay["off_xr"]:lay["off_xr"] + d_rnn]
        gr_s = z_s[:, :, lay["off_gr"]:lay["off_gr"] + d_rnn]
        o_rnn_s, h_last_s = rnn_sample(jnp.swapaxes(xr_s, 0, 1), jnp.swapaxes(state_conv[l], 0, 1),
                                       jnp.swapaxes(gr_s, 0, 1), state_h[l], conv_w[l], conv_b[l],
                                       w["w_ra"][l], b_ra[l], w["w_ri"][l], b_ri[l], lru_lambda[l])
        o_rnn_s = jnp.swapaxes(o_rnn_s, 0, 1).reshape(n_s, d_rnn)
        x = matmul_res2(jnp.concatenate([o_attn_p, o_attn_s], axis=0),
                        jnp.concatenate([o_rnn_p, o_rnn_s], axis=0), w["w_o"], l, x)
        f = rmsnorm(x, norm_ffn[l], BF16)
        x = matmul_kres(swiglu(f, w["wg"], w["wu"], l), w["wd"], l, x)

        outs["p_ckv"].append(ckv[:n_p].reshape(batch, seq, kv_rank))
        outs["p_kpe"].append(kpe[:n_p].reshape(batch, seq, rope))
        outs["p_h"].append(h_last_p)
        outs["p_conv"].append(conv_p)
        outs["s_ckv"].append(ckv[n_p:].reshape(n_seq, dec_seq, kv_rank))
        outs["s_kpe"].append(kpe[n_p:].reshape(n_seq, dec_seq, rope))
        outs["s_h"].append(h_last_s)
        outs["s_conv"].append(xr_s[:, dec_seq - (cwid - 1):])

    y_prompt = rmsnorm(x, norm_final, F32, 0, n_p).reshape(batch, seq, d_model)
    y_sample = rmsnorm(x, norm_final, F32, n_p, n_s).reshape(n_seq, dec_seq, d_model)
    st = lambda k: jnp.stack(outs[k])
    return (y_prompt, y_sample, st("p_ckv"), st("p_kpe"), st("p_h"), st("p_conv"),
            st("s_ckv"), st("s_kpe"), st("s_h"), st("s_conv"))
```

```python
import functools

import jax
import jax.numpy as jnp
from jax import lax
from jax.experimental import pallas as pl
from jax.experimental.pallas import tpu as pltpu

EPS = 1e-6
ROPE_THETA = 10000.0
LRU_C = 8.0
LOG2E = 1.4426950408889634

LANES = 128
ROPE_GROUP = 128
VMEM_LIMIT = 56 * 1024 * 1024
MXU_GROUP_ROWS = 256

F32 = jnp.float32
BF16 = jnp.bfloat16


def _tile(n, pref, mult):
    best = None
    for t in range(mult, min(n, pref) + 1, mult):
        if n % t == 0:
            best = t
    return best if best is not None else n


def _params(sem):
    return pltpu.CompilerParams(dimension_semantics=sem, vmem_limit_bytes=VMEM_LIMIT)


def _dot(a, b):
    return jnp.dot(a, b, preferred_element_type=F32)


def _dot_nt(a, b):
    return lax.dot_general(a, b, (((1,), (1,)), ((), ())), preferred_element_type=F32)


def _rmsnorm_kernel(x_ref, g_ref, o_ref):
    x = x_ref[...]
    y = x * lax.rsqrt(jnp.mean(x * x, axis=-1, keepdims=True) + EPS)
    o_ref[...] = (y * g_ref[...]).astype(o_ref.dtype)


def rmsnorm(x, g, out_dtype, row0=0, rows=None):
    m, d = x.shape
    rows = m if rows is None else rows
    tm = _tile(rows, 512, 16)
    assert row0 % tm == 0
    off = row0 // tm
    return pl.pallas_call(
        _rmsnorm_kernel,
        grid=(rows // tm,),
        in_specs=[pl.BlockSpec((tm, d), lambda i: (i + off, 0)),
                  pl.BlockSpec((1, d), lambda i: (0, 0))],
        out_specs=pl.BlockSpec((tm, d), lambda i: (i, 0)),
        out_shape=jax.ShapeDtypeStruct((rows, d), out_dtype),
        compiler_params=_params(("parallel",)),
        name="rmsnorm",
    )(x, g.reshape(1, d))


def _mm_kernel(x_ref, w_ref, o_ref):
    o_ref[...] = _dot(x_ref[...], w_ref[...]).astype(o_ref.dtype)


def _mm_res2_kernel(xa_ref, xb_ref, wa_ref, wb_ref, r_ref, o_ref):
    o_ref[...] = r_ref[...] + (_dot(xa_ref[...], wa_ref[...]) + _dot(xb_ref[...], wb_ref[...]))


def _swiglu_kernel(x_ref, wg_ref, wu_ref, o_ref):
    x = x_ref[...]
    g = _dot(x, wg_ref[...])
    u = _dot(x, wu_ref[...])
    o_ref[...] = (jax.nn.silu(g) * u).astype(o_ref.dtype)


def _mm_kres_kernel(x_ref, w_ref, r_ref, o_ref, acc_ref):
    k = pl.program_id(2)

    @pl.when(k == 0)
    def _():
        acc_ref[...] = jnp.zeros_like(acc_ref)

    acc_ref[...] += _dot(x_ref[...], w_ref[...])

    @pl.when(k == pl.num_programs(2) - 1)
    def _():
        o_ref[...] = r_ref[...] + acc_ref[...]


def matmul(x, w, layer, out_dtype, tm_pref=1088, tn_pref=1024):
    m, k = x.shape
    n = w.shape[2]
    tm, tn = _tile(m, tm_pref, 16), _tile(n, tn_pref, LANES)
    return pl.pallas_call(
        _mm_kernel,
        grid=(m // tm, n // tn),
        in_specs=[pl.BlockSpec((tm, k), lambda i, j: (i, 0)),
                  pl.BlockSpec((None, k, tn), lambda i, j: (layer, 0, j))],
        out_specs=pl.BlockSpec((tm, tn), lambda i, j: (i, j)),
        out_shape=jax.ShapeDtypeStruct((m, n), out_dtype),
        compiler_params=_params(("parallel", "parallel")),
        name="matmul",
    )(x, w)


def matmul_res2(xa, xb, w, layer, res, tm_pref=1088, tn_pref=512):
    m, ka = xa.shape
    kb = xb.shape[1]
    n = w.shape[2]
    assert ka == kb and w.shape[1] == ka + kb
    tm, tn = _tile(m, tm_pref, 16), _tile(n, tn_pref, LANES)
    return pl.pallas_call(
        _mm_res2_kernel,
        grid=(m // tm, n // tn),
        in_specs=[pl.BlockSpec((tm, ka), lambda i, j: (i, 0)),
                  pl.BlockSpec((tm, kb), lambda i, j: (i, 0)),
                  pl.BlockSpec((None, ka, tn), lambda i, j: (layer, 0, j)),
                  pl.BlockSpec((None, kb, tn), lambda i, j: (layer, 1, j)),
                  pl.BlockSpec((tm, tn), lambda i, j: (i, j))],
        out_specs=pl.BlockSpec((tm, tn), lambda i, j: (i, j)),
        out_shape=jax.ShapeDtypeStruct((m, n), F32),
        compiler_params=_params(("parallel", "parallel")),
        name="matmul_res2",
    )(xa, xb, w, w, res)


def swiglu(x, wg, wu, layer, tm_pref=1088, tn_pref=512):
    m, k = x.shape
    n = wg.shape[2]
    tm, tn = _tile(m, tm_pref, 16), _tile(n, tn_pref, LANES)
    w_spec = pl.BlockSpec((None, k, tn), lambda i, j: (layer, 0, j))
    return pl.pallas_call(
        _swiglu_kernel,
        grid=(m // tm, n // tn),
        in_specs=[pl.BlockSpec((tm, k), lambda i, j: (i, 0)), w_spec, w_spec],
        out_specs=pl.BlockSpec((tm, tn), lambda i, j: (i, j)),
        out_shape=jax.ShapeDtypeStruct((m, n), BF16),
        compiler_params=_params(("parallel", "parallel")),
        name="swiglu",
    )(x, wg, wu)


def matmul_kres(x, w, layer, res, tm_pref=1088, tn_pref=512, tk_pref=5504):
    m = x.shape[0]
    k, n = w.shape[1], w.shape[2]
    assert x.shape[1] >= k
    tm, tn, tk = _tile(m, tm_pref, 16), _tile(n, tn_pref, LANES), _tile(k, tk_pref, LANES)
    return pl.pallas_call(
        _mm_kres_kernel,
        grid=(m // tm, n // tn, k // tk),
        in_specs=[pl.BlockSpec((tm, tk), lambda i, j, kk: (i, kk)),
                  pl.BlockSpec((None, tk, tn), lambda i, j, kk: (layer, kk, j)),
                  pl.BlockSpec((tm, tn), lambda i, j, kk: (i, j))],
        out_specs=pl.BlockSpec((tm, tn), lambda i, j, kk: (i, j)),
        out_shape=jax.ShapeDtypeStruct((m, n), F32),
        scratch_shapes=[pltpu.VMEM((tm, tn), F32)],
        compiler_params=_params(("parallel", "parallel", "arbitrary")),
        name="matmul_kres",
    )(x, w, res)


def _rope_group(x, cs, sn):
    return x * cs + pltpu.roll(x, ROPE_GROUP // 4, 1) * sn


def _mla_prep_kernel(qc_ref, kvc_ref, kpe_ref, cs_ref, sn_ref, qg_ref, kvg_ref,
                     wqn_ref, wqp_ref, wuk_ref,
                     qcat_ref, ckv_ref, kpeo_ref, kcat_ref, *, n_heads, nope, kv_rank, rope):
    tq = qcat_ref.shape[2]
    n_sub = qcat_ref.shape[0]
    cs = cs_ref[...]
    sn = sn_ref[...]

    qc = qc_ref[...]
    qn = (qc * lax.rsqrt(jnp.mean(qc * qc, axis=-1, keepdims=True) + EPS) * qg_ref[...]).astype(BF16)
    q_nope = _dot(qn, wqn_ref[...])
    q_pe = _dot(qn, wqp_ref[...])
    for h in range(n_heads):
        ql = _dot(q_nope[:, h * nope:(h + 1) * nope].astype(BF16), wuk_ref[h]).astype(BF16)
        rp = _rope_group(q_pe[:, h * ROPE_GROUP:(h + 1) * ROPE_GROUP], cs, sn).astype(BF16)
        for t in range(n_sub):
            qcat_ref[t, h, :, 0:kv_rank] = ql[t * tq:(t + 1) * tq]
            qcat_ref[t, h, :, kv_rank:kv_rank + ROPE_GROUP] = rp[t * tq:(t + 1) * tq]

    kvc = kvc_ref[...]
    ckv = kvc * lax.rsqrt(jnp.mean(kvc * kvc, axis=-1, keepdims=True) + EPS) * kvg_ref[...]
    kr = _rope_group(kpe_ref[...], cs, sn)
    ckv_ref[...] = ckv
    kpeo_ref[...] = kr[:, 0:rope]
    kcat_ref[:, 0:kv_rank] = ckv.astype(BF16)
    kcat_ref[:, kv_rank:kv_rank + ROPE_GROUP] = kr.astype(BF16)


def mla_prep(z, lay, cs, sn, q_norm, kv_norm, wq_nope, wq_pe, wuk_t, layer, tq):
    nt = z.shape[0]
    qr, kv, rope = lay["q_rank"], lay["kv_rank"], lay["rope"]
    n_heads, nope = wuk_t.shape[1], wuk_t.shape[2]
    dq = kv + ROPE_GROUP
    tm = 2 * tq if nt % (2 * tq) == 0 else tq
    n_sub = tm // tq
    assert nt % tm == 0 and qr % kv == 0 and lay["off_kpe"] % ROPE_GROUP == 0
    kern = functools.partial(_mla_prep_kernel, n_heads=n_heads, nope=nope, kv_rank=kv, rope=rope)
    const2 = lambda i: (0, 0)
    return pl.pallas_call(
        kern,
        grid=(nt // tm,),
        in_specs=[pl.BlockSpec((tm, qr), lambda i: (i, 0)),
                  pl.BlockSpec((tm, kv), lambda i: (i, qr // kv)),
                  pl.BlockSpec((tm, ROPE_GROUP), lambda i: (i, lay["off_kpe"] // ROPE_GROUP)),
                  pl.BlockSpec((tm, ROPE_GROUP), lambda i: (i, 0)),
                  pl.BlockSpec((tm, ROPE_GROUP), lambda i: (i, 0)),
                  pl.BlockSpec((1, qr), const2),
                  pl.BlockSpec((1, kv), const2),
                  pl.BlockSpec((None,) + wq_nope.shape[1:], lambda i: (layer, 0, 0)),
                  pl.BlockSpec((None,) + wq_pe.shape[1:], lambda i: (layer, 0, 0)),
                  pl.BlockSpec((None,) + wuk_t.shape[1:], lambda i: (layer, 0, 0, 0))],
        out_specs=[pl.BlockSpec((n_sub, n_heads, tq, dq), lambda i: (i, 0, 0, 0)),
                   pl.BlockSpec((tm, kv), lambda i: (i, 0)),
                   pl.BlockSpec((tm, rope), lambda i: (i, 0)),
                   pl.BlockSpec((tm, dq), lambda i: (i, 0))],
        out_shape=[jax.ShapeDtypeStruct((nt // tq, n_heads, tq, dq), BF16),
                   jax.ShapeDtypeStruct((nt, kv), F32),
                   jax.ShapeDtypeStruct((nt, rope), F32),
                   jax.ShapeDtypeStruct((nt, dq), BF16)],
        compiler_params=_params(("parallel",)),
        name="mla_prep",
    )(z, z, z, cs, sn, q_norm.reshape(1, qr), kv_norm.reshape(1, kv), wq_nope, wq_pe, wuk_t)


def _prompt_attn_kernel(q_ref, k_ref, wuv_ref, o_ref, m_ref, l_ref, acc_ref, s_ref, p_ref, alpha_ref,
                        *, tq, tk, kv_rank, scale, chunk_rows, n_groups):
    i = pl.program_id(1)
    j = pl.program_id(2)
    n_heads = q_ref.shape[1]
    rows = n_heads * tq

    @pl.when(j == 0)
    def _():
        m_ref[...] = jnp.full_like(m_ref, -jnp.inf)
        l_ref[...] = jnp.zeros_like(l_ref)
        acc_ref[...] = jnp.zeros_like(acc_ref)

    last = ((i + 1) * tq - 1) // tk
    c2 = scale * LOG2E

    def block(masked):
        k = k_ref[...]
        v = k[:, 0:kv_rank]
        g_rows = rows // n_groups
        hpg = n_heads // n_groups
        for g in range(n_groups):
            gs = slice(g * g_rows, (g + 1) * g_rows)
            s_ref[gs, :] = _dot_nt(q_ref[0, g * hpg:(g + 1) * hpg].reshape(g_rows, q_ref.shape[3]), k)
        for g in range(n_groups):
            gs = slice(g * g_rows, (g + 1) * g_rows)
            for c in range(g_rows // chunk_rows):
                r0 = g * g_rows + c * chunk_rows
                rs = slice(r0, r0 + chunk_rows)
                s = s_ref[rs, :]
                if masked:
                    row = r0 + lax.broadcasted_iota(jnp.int32, (chunk_rows, 1), 0)
                    q_pos = i * tq + (row & (tq - 1))
                    k_pos = j * tk + lax.broadcasted_iota(jnp.int32, (1, tk), 1)
                    s = jnp.where(k_pos <= q_pos, s, -jnp.inf)
                m_prev = m_ref[rs, :]
                m_new = jnp.maximum(m_prev, jnp.max(s, axis=-1, keepdims=True))
                alpha = jnp.exp2((m_prev - m_new) * c2)
                p = jnp.exp2((s - m_new) * c2)
                l_ref[rs, :] = alpha * l_ref[rs, :] + jnp.sum(p, axis=-1, keepdims=True)
                m_ref[rs, :] = m_new
                alpha_ref[rs, :] = alpha
                p_ref[rs, :] = p.astype(BF16)
            acc_ref[gs, :] = alpha_ref[gs, :] * acc_ref[gs, :] + _dot(p_ref[gs, :], v)

    pl.when(j < last)(functools.partial(block, False))
    pl.when(j == last)(functools.partial(block, True))

    @pl.when(j == pl.num_programs(2) - 1)
    def _():
        vh = wuv_ref.shape[2]
        for h in range(n_heads):
            sl = slice(h * tq, (h + 1) * tq)
            o_lat = (acc_ref[sl, :] / l_ref[sl, :]).astype(BF16)
            o_ref[:, h * vh:(h + 1) * vh] = _dot(o_lat, wuv_ref[h]).astype(o_ref.dtype)


def prompt_attention(qcat, kcat, wuv, layer, batch, seq, kv_rank, scale):
    n_heads, tq, dq = qcat.shape[1], qcat.shape[2], qcat.shape[3]
    assert tq & (tq - 1) == 0 and seq % tq == 0
    tk = _tile(seq, 512, tq)
    nq, nk = seq // tq, seq // tk
    vh = wuv.shape[3]
    rows = n_heads * tq
    chunk_rows = 128
    kern = functools.partial(_prompt_attn_kernel, tq=tq, tk=tk, kv_rank=kv_rank, scale=scale,
                             chunk_rows=chunk_rows, n_groups=max(1, rows // MXU_GROUP_ROWS))

    def k_map(b, i, j):
        return (b * nk + jnp.minimum(j, ((i + 1) * tq - 1) // tk), 0)

    return pl.pallas_call(
        kern,
        grid=(batch, nq, nk),
        in_specs=[pl.BlockSpec((1, n_heads, tq, dq), lambda b, i, j: (b * nq + i, 0, 0, 0)),
                  pl.BlockSpec((tk, dq), k_map),
                  pl.BlockSpec((None,) + wuv.shape[1:], lambda b, i, j: (layer, 0, 0, 0))],
        out_specs=pl.BlockSpec((tq, n_heads * vh), lambda b, i, j: (b * nq + i, 0)),
        out_shape=jax.ShapeDtypeStruct((batch * seq, n_heads * vh), BF16),
        scratch_shapes=[pltpu.VMEM((rows, 1), F32), pltpu.VMEM((rows, 1), F32),
                        pltpu.VMEM((rows, kv_rank), F32),
                        pltpu.VMEM((rows, tk), F32), pltpu.VMEM((rows, tk), BF16), pltpu.VMEM((rows, 1), F32)],
        compiler_params=_params(("parallel", "parallel", "arbitrary")),
        name="prompt_attention",
    )(qcat, kcat, wuv)


def _sample_attn_kernel(pt_ref, q_ref, kn_ref, ckv_hbm, kpe_hbm, o_ref,
                        ckv_in, kpe_in, kc_buf, kp_buf, s_ref, m_ref, l_ref, acc_ref, sem,
                        *, layer, n_pages_step, pages_per_group, page, kv_rank, rope, dec_seq, scale):
    b = pl.program_id(0)
    c = pl.program_id(1)
    n_chunks = pl.num_programs(1)
    step = b * n_chunks + c
    slot = step % 2

    def page_copies(bb, cc, sl):
        out = []
        for p_i in range(n_pages_step):
            pg = pt_ref[bb, cc * n_pages_step + p_i]
            out.append(pltpu.make_async_copy(ckv_hbm.at[layer, pg], ckv_in.at[sl, p_i], sem.at[sl]))
            out.append(pltpu.make_async_copy(kpe_hbm.at[layer, pg], kpe_in.at[sl, p_i], sem.at[sl]))
        return out

    def start_all(copies):
        for n, cp in enumerate(copies):
            cp.start(priority=(n // 2) % 2)

    @pl.when(step == 0)
    def _():
        start_all(page_copies(b, c, slot))

    @pl.when(step + 1 < pl.num_programs(0) * n_chunks)
    def _():
        wrap = c + 1 == n_chunks
        start_all(page_copies(jnp.where(wrap, b + 1, b), jnp.where(wrap, 0, c + 1), 1 - slot))

    @pl.when(c == 0)
    def _():
        m_ref[...] = jnp.full_like(m_ref, -jnp.inf)
        l_ref[...] = jnp.zeros_like(l_ref)
        acc_ref[...] = jnp.zeros_like(acc_ref)
        kp_buf[rope:ROPE_GROUP, :] = jnp.zeros((ROPE_GROUP - rope, kp_buf.shape[1]), BF16)

    q = q_ref[0]
    q_lat = q[:, 0:kv_rank]
    q_pe = q[:, kv_rank:kv_rank + ROPE_GROUP]

    def update(s, pv_fn):
        m_prev = m_ref[...]
        m_new = jnp.maximum(m_prev, jnp.max(s, axis=-1, keepdims=True))
        alpha = jnp.exp(m_prev - m_new)
        p = jnp.exp(s - m_new)
        l_ref[...] = alpha * l_ref[...] + jnp.sum(p, axis=-1, keepdims=True)
        acc_ref[...] = alpha * acc_ref[...] + pv_fn(p.astype(BF16))
        m_ref[...] = m_new

    for cp in page_copies(b, c, slot):
        cp.wait()
    g_keys = pages_per_group * page
    n_groups = n_pages_step // pages_per_group
    for g in range(n_groups):
        for p_i in range(g * pages_per_group, (g + 1) * pages_per_group):
            kc_buf[p_i * page:(p_i + 1) * page, :] = ckv_in[slot, p_i].astype(BF16)
            kp_buf[0:rope, p_i * page:(p_i + 1) * page] = kpe_in[slot, p_i].astype(BF16)
        ks = slice(g * g_keys, (g + 1) * g_keys)
        s_ref[:, ks] = (_dot_nt(q_lat, kc_buf[ks, :]) + _dot(q_pe, kp_buf[:, ks])) * scale

    def pv_cached(p_bf):
        out = None
        for g in range(n_groups):
            ks = slice(g * g_keys, (g + 1) * g_keys)
            term = _dot(p_bf[:, ks], kc_buf[ks, :])
            out = term if out is None else out + term
        return out

    update(s_ref[...], pv_cached)

    @pl.when(c == pl.num_programs(1) - 1)
    def _():
        kn = kn_ref[0]
        kn_lat = kn[:, 0:kv_rank]
        s_new = (_dot_nt(q_lat, kn_lat) + _dot_nt(q_pe, kn[:, kv_rank:kv_rank + ROPE_GROUP])) * scale
        rows = q.shape[0]
        q_t = lax.broadcasted_iota(jnp.int32, (rows, 1), 0) % dec_seq
        k_t = lax.broadcasted_iota(jnp.int32, (1, kn.shape[0]), 1)
        update(jnp.where(k_t <= q_t, s_new, -jnp.inf),
               lambda p_bf: _dot(p_bf, kn_lat))
        o_ref[0] = (acc_ref[...] / l_ref[...]).astype(o_ref.dtype)


def sample_attention(q_s, k_new, cache_ckv, cache_kpe_t, page_table, layer, scale, pages_per_step=32):
    n_seq, rows, dq = q_s.shape
    dec_seq = k_new.shape[1]
    k_new = jnp.pad(k_new, ((0, 0), (0, LANES - dec_seq), (0, 0)))
    page, kv_rank = cache_ckv.shape[2], cache_ckv.shape[3]
    rope = cache_kpe_t.shape[2]
    n_pages = page_table.shape[1]
    pps = _tile(n_pages, pages_per_step, 1)
    ppg = _tile(pps, 4, 1)
    kern = functools.partial(_sample_attn_kernel, layer=layer, n_pages_step=pps, pages_per_group=ppg, page=page,
                             kv_rank=kv_rank, rope=rope, dec_seq=dec_seq, scale=scale)
    grid_spec = pltpu.PrefetchScalarGridSpec(
        num_scalar_prefetch=1,
        grid=(n_seq, n_pages // pps),
        in_specs=[pl.BlockSpec((1, rows, dq), lambda b, c, pt: (b, 0, 0)),
                  pl.BlockSpec((1, LANES, dq), lambda b, c, pt: (b, 0, 0)),
                  pl.BlockSpec(memory_space=pl.ANY),
                  pl.BlockSpec(memory_space=pl.ANY)],
        out_specs=pl.BlockSpec((1, rows, kv_rank), lambda b, c, pt: (b, 0, 0)),
        scratch_shapes=[pltpu.VMEM((2, pps, page, kv_rank), F32),
                        pltpu.VMEM((2, pps, rope, page), F32),
                        pltpu.VMEM((pps * page, kv_rank), BF16),
                        pltpu.VMEM((ROPE_GROUP, pps * page), BF16),
                        pltpu.VMEM((rows, pps * page), F32),
                        pltpu.VMEM((rows, 1), F32), pltpu.VMEM((rows, 1), F32),
                        pltpu.VMEM((rows, kv_rank), F32),
                        pltpu.SemaphoreType.DMA((2,))])
    return pl.pallas_call(
        kern,
        grid_spec=grid_spec,
        out_shape=jax.ShapeDtypeStruct((n_seq, rows, kv_rank), BF16),
        compiler_params=_params(("arbitrary", "arbitrary")),
        name="sample_attention",
    )(page_table, q_s, k_new, cache_ckv, cache_kpe_t)


def _uv_proj_kernel(o_ref, w_ref, out_ref):
    out_ref[...] = _dot(o_ref[0], w_ref[0]).astype(out_ref.dtype)


def uv_proj(o_lat_hm, wuv, layer):
    n_heads, n, kv_rank = o_lat_hm.shape
    vh = wuv.shape[3]
    return pl.pallas_call(
        _uv_proj_kernel,
        grid=(n_heads,),
        in_specs=[pl.BlockSpec((1, n, kv_rank), lambda h: (h, 0, 0)),
                  pl.BlockSpec((None, 1, kv_rank, vh), lambda h: (layer, h, 0, 0))],
        out_specs=pl.BlockSpec((n, vh), lambda h: (0, h)),
        out_shape=jax.ShapeDtypeStruct((n, n_heads * vh), BF16),
        compiler_params=_params(("parallel",)),
        name="uv_proj",
    )(o_lat_hm, wuv)


def _lru_gates(xc, wra, bra, wri, bri, lam):
    xb = xc.astype(BF16)
    r = jax.nn.sigmoid(_dot(xb, wra) + bra)
    i = jax.nn.sigmoid(_dot(xb, wri) + bri)
    log_a = -LRU_C * r * jax.nn.softplus(-lam)
    a = jnp.exp(log_a)
    one_minus_a2 = -jnp.tanh(log_a) * (a * a + 1.0)
    u = jnp.sqrt(one_minus_a2) * (i * xc)
    return a, u


def _rnn_gates_prompt_kernel(x_ref, cw_ref, cb_ref, wra_ref, bra_ref, wri_ref, bri_ref, lam_ref,
                             a_ref, u_ref, *, conv_w):
    x = x_ref[...]
    t_idx = lax.broadcasted_iota(jnp.int32, (x.shape[0], 1), 0)
    cw = cw_ref[...]
    acc = None
    for k in range(conv_w):
        d = conv_w - 1 - k
        xs = x if d == 0 else jnp.where(t_idx >= d, pltpu.roll(x, d, 0), 0.0)
        term = xs * cw[k:k + 1, :]
        acc = term if acc is None else acc + term
    xc = cb_ref[...] + acc
    a, u = _lru_gates(xc, wra_ref[0], bra_ref[...], wri_ref[0], bri_ref[...], lam_ref[...])
    a_ref[...] = a
    u_ref[...] = u


def rnn_gates_prompt(z, lay, batch, seq, conv_w, conv_b, wra, bra, wri, bri, lam):
    d_rnn = lay["d_rnn"]
    nb, cb = wra.shape[0], wra.shape[1]
    assert cb == LANES
    x_off = lay["off_xr"] // cb
    kern = functools.partial(_rnn_gates_prompt_kernel, conv_w=conv_w.shape[0])
    vec = lambda b, n: (0, n)
    blk = lambda b, n: (n, 0, 0)
    a, u = pl.pallas_call(
        kern,
        grid=(batch, nb),
        in_specs=[pl.BlockSpec((seq, cb), lambda b, n: (b, x_off + n)),
                  pl.BlockSpec((conv_w.shape[0], cb), vec),
                  pl.BlockSpec((1, cb), vec),
                  pl.BlockSpec((1, cb, cb), blk), pl.BlockSpec((1, cb), vec),
                  pl.BlockSpec((1, cb, cb), blk), pl.BlockSpec((1, cb), vec),
                  pl.BlockSpec((1, cb), vec)],
        out_specs=[pl.BlockSpec((seq, cb), lambda b, n: (b, n))] * 2,
        out_shape=[jax.ShapeDtypeStruct((batch * seq, d_rnn), F32)] * 2,
        compiler_params=_params(("parallel", "parallel")),
        name="rnn_gates_prompt",
    )(z, conv_w, conv_b.reshape(1, d_rnn), wra, bra.reshape(1, d_rnn), wri, bri.reshape(1, d_rnn),
      lam.reshape(1, d_rnn))
    return a, u


def _rnn_scan_prompt_kernel(a_ref, u_ref, g_ref, o_ref, hl_ref, h_ref, hs_ref):
    tt = pl.program_id(1)

    @pl.when(tt == 0)
    def _():
        h_ref[...] = jnp.zeros_like(h_ref)

    def step(t, h):
        h = a_ref[t] * h + u_ref[t]
        hs_ref[t] = h
        return h

    h = lax.fori_loop(0, a_ref.shape[0], step, h_ref[...], unroll=8)
    h_ref[...] = h
    o_ref[...] = (hs_ref[...] * jax.nn.gelu(g_ref[...])).astype(o_ref.dtype)

    @pl.when(tt == pl.num_programs(1) - 1)
    def _():
        hl_ref[0] = h


def rnn_scan_prompt(a, u, z, lay, batch, seq):
    d_rnn = lay["d_rnn"]
    nb = d_rnn // LANES
    assert lay["off_gr"] % d_rnn == 0 and z.shape[1] % LANES == 0
    tc = _tile(seq, 512, 8)
    nt = seq // tc
    a4 = a.reshape(batch * seq, nb, LANES)
    u4 = u.reshape(batch * seq, nb, LANES)
    z4 = z.reshape(z.shape[0], z.shape[1] // LANES, LANES)
    g_blk = lay["off_gr"] // d_rnn
    row = lambda b, t: (b * nt + t, 0, 0)
    o, hl = pl.pallas_call(
        _rnn_scan_prompt_kernel,
        grid=(batch, nt),
        in_specs=[pl.BlockSpec((tc, nb, LANES), row),
                  pl.BlockSpec((tc, nb, LANES), row),
                  pl.BlockSpec((tc, nb, LANES), lambda b, t: (b * nt + t, g_blk, 0))],
        out_specs=[pl.BlockSpec((tc, nb, LANES), row),
                   pl.BlockSpec((1, nb, LANES), lambda b, t: (b, 0, 0))],
        out_shape=[jax.ShapeDtypeStruct((batch * seq, nb, LANES), BF16),
                   jax.ShapeDtypeStruct((batch, nb, LANES), F32)],
        scratch_shapes=[pltpu.VMEM((nb, LANES), F32), pltpu.VMEM((tc, nb, LANES), F32)],
        compiler_params=_params(("parallel", "arbitrary")),
        name="rnn_scan_prompt",
    )(a4, u4, z4)
    return o.reshape(batch * seq, d_rnn), hl.reshape(batch, d_rnn)


def _rnn_sample_kernel(x_ref, st_ref, g_ref, h0_ref, cw_ref, cb_ref, wra_ref, bra_ref, wri_ref, bri_ref,
                       lam_ref, o_ref, hl_ref, *, conv_w):
    dec_seq, n_seq = x_ref.shape[0], x_ref.shape[1]
    cw = cw_ref[...]
    xp = [st_ref[k] for k in range(conv_w - 1)] + [x_ref[t] for t in range(dec_seq)]
    xc = []
    for t in range(dec_seq):
        acc = None
        for k in range(conv_w):
            term = xp[t + k] * cw[k:k + 1, :]
            acc = term if acc is None else acc + term
        xc.append(cb_ref[...] + acc)
    a, u = _lru_gates(jnp.concatenate(xc, axis=0), wra_ref[0], bra_ref[...], wri_ref[0], bri_ref[...],
                      lam_ref[...])
    h = h0_ref[...]
    for t in range(dec_seq):
        h = a[t * n_seq:(t + 1) * n_seq] * h + u[t * n_seq:(t + 1) * n_seq]
        o_ref[t] = (h * jax.nn.gelu(g_ref[t])).astype(o_ref.dtype)
    hl_ref[...] = h


def rnn_sample(x_t, st_t, g_t, h0, conv_w, conv_b, wra, bra, wri, bri, lam):
    dec_seq, n_seq, d_rnn = x_t.shape
    nb, cb = wra.shape[0], wra.shape[1]
    cwid = conv_w.shape[0]
    kern = functools.partial(_rnn_sample_kernel, conv_w=cwid)
    vec = lambda n: (0, n)
    blk = lambda n: (n, 0, 0)
    t3 = lambda n: (0, 0, n)
    return pl.pallas_call(
        kern,
        grid=(nb,),
        in_specs=[pl.BlockSpec((dec_seq, n_seq, cb), t3),
                  pl.BlockSpec((cwid - 1, n_seq, cb), t3),
                  pl.BlockSpec((dec_seq, n_seq, cb), t3),
                  pl.BlockSpec((n_seq, cb), vec),
                  pl.BlockSpec((cwid, cb), vec), pl.BlockSpec((1, cb), vec),
                  pl.BlockSpec((1, cb, cb), blk), pl.BlockSpec((1, cb), vec),
                  pl.BlockSpec((1, cb, cb), blk), pl.BlockSpec((1, cb), vec),
                  pl.BlockSpec((1, cb), vec)],
        out_specs=[pl.BlockSpec((dec_seq, n_seq, cb), t3),
                   pl.BlockSpec((n_seq, cb), vec)],
        out_shape=[jax.ShapeDtypeStruct((dec_seq, n_seq, d_rnn), BF16),
                   jax.ShapeDtypeStruct((n_seq, d_rnn), F32)],
        compiler_params=_params(("parallel",)),
        name="rnn_sample",
    )(x_t, st_t, g_t, h0, conv_w, conv_b.reshape(1, d_rnn), wra, bra.reshape(1, d_rnn),
      wri, bri.reshape(1, d_rnn), lam.reshape(1, d_rnn))


def _cast_pad_kernel(x_ref, o_ref, *, n_valid_blocks):
    j = pl.program_id(2)

    @pl.when(j < n_valid_blocks)
    def _():
        o_ref[...] = x_ref[...].astype(o_ref.dtype)

    @pl.when(j >= n_valid_blocks)
    def _():
        o_ref[...] = jnp.zeros_like(o_ref)


def cast_pad_cols(w, n_pad, tc=256):
    depth, k, n = w.shape
    assert n % tc == 0 and n_pad % tc == 0
    tr = _tile(k, 2048, 16)
    nvb = n // tc
    return pl.pallas_call(
        functools.partial(_cast_pad_kernel, n_valid_blocks=nvb),
        grid=(depth, k // tr, n_pad // tc),
        in_specs=[pl.BlockSpec((None, tr, tc), lambda l, i, j: (l, i, jnp.minimum(j, nvb - 1)))],
        out_specs=pl.BlockSpec((None, tr, tc), lambda l, i, j: (l, i, j)),
        out_shape=jax.ShapeDtypeStruct((depth, k, n_pad), BF16),
        compiler_params=_params(("parallel", "parallel", "arbitrary")),
        name="cast_pad_cols",
    )(w)


def _z_layout(q_rank, kv_rank, rope, d_rnn):
    off_kpe = q_rank + kv_rank
    off_xr = -(-(off_kpe + ROPE_GROUP) // d_rnn) * d_rnn
    off_gr = off_xr + d_rnn
    return dict(q_rank=q_rank, kv_rank=kv_rank, rope=rope, d_rnn=d_rnn, off_kpe=off_kpe,
                off_xr=off_xr, off_gr=off_gr, width=off_gr + d_rnn)


def _dup_rope_cols(w, rope):
    return jnp.concatenate([w, w], axis=-1)


def _prep_weights(lay, w_in, w_uq, w_uk, w_uv, w_ra, w_ri, w_o, w_gate, w_up, w_down, d_ff_pad):
    q_rank, kv_rank, rope = lay["q_rank"], lay["kv_rank"], lay["rope"]
    depth, d_model = w_in.shape[0], w_in.shape[1]
    n_heads, nope = w_uk.shape[2], w_uk.shape[3]
    o = q_rank + kv_rank
    pad = lay["off_xr"] - (lay["off_kpe"] + ROPE_GROUP)
    w_in = w_in.astype(BF16)
    w_in_p = jnp.concatenate(
        [w_in[:, :, :o], _dup_rope_cols(w_in[:, :, o:o + rope], rope), jnp.zeros((depth, d_model, pad), BF16),
         w_in[:, :, o + rope:]], axis=2)
    wq = w_uq.astype(BF16).reshape(depth, q_rank, n_heads, nope + rope)
    wq_nope = wq[..., :nope].reshape(depth, q_rank, n_heads * nope)
    wq_pe = _dup_rope_cols(wq[..., nope:], rope).reshape(depth, q_rank, n_heads * ROPE_GROUP)
    wuk_t = jnp.transpose(w_uk, (0, 2, 3, 1)).astype(BF16)
    wuv_h = jnp.transpose(w_uv, (0, 2, 1, 3)).astype(BF16)
    return dict(w_in=w_in_p, wq_nope=wq_nope, wq_pe=wq_pe, wuk_t=wuk_t, wuv=wuv_h,
                w_ra=w_ra.astype(BF16), w_ri=w_ri.astype(BF16), w_o=w_o.astype(BF16),
                wg=cast_pad_cols(w_gate, d_ff_pad), wu=cast_pad_cols(w_up, d_ff_pad), wd=w_down.astype(BF16))


def _rope_tables(pos, rope):
    half = rope // 2
    inv = ROPE_THETA ** (-jnp.arange(half, dtype=F32) / half)
    ang = pos.astype(F32)[:, None] * inv[None, :]
    c, s = jnp.cos(ang), jnp.sin(ang)
    zero = jnp.zeros_like(c)
    return (jnp.concatenate([c, c, zero, zero], axis=1),
            jnp.concatenate([-s, s, zero, zero], axis=1))


def kernel(x_prompt, x_sample, cache_ckv, cache_kpe, state_h, state_conv, page_table, norm_mix, w_in, q_norm, kv_norm, w_uq, w_uk, w_uv, conv_w, conv_b, w_ra, b_ra, w_ri, b_ri, lru_lambda, w_o, norm_ffn, w_gate, w_up, w_down, norm_final):
    batch, seq, d_model = x_prompt.shape
    n_seq, dec_seq, _ = x_sample.shape
    depth = norm_mix.shape[0]
    q_rank, kv_rank = q_norm.shape[1], kv_norm.shape[1]
    rope = cache_kpe.shape[3]
    n_heads, nope = w_uk.shape[2], w_uk.shape[3]
    d_rnn = state_h.shape[2]
    cwid = conv_w.shape[1]
    d_ff = w_gate.shape[2]
    past_len = page_table.shape[1] * cache_ckv.shape[2]
    assert 4 * (rope // 2) == ROPE_GROUP and dec_seq >= cwid - 1 and seq >= cwid - 1
    scale = float(nope + rope) ** -0.5
    lay = _z_layout(q_rank, kv_rank, rope, d_rnn)
    d_ff_pad = -(-d_ff // 1024) * 1024 if d_ff > 1024 else d_ff
    tq = 128

    n_p, n_s = batch * seq, n_seq * dec_seq
    nt = n_p + n_s
    assert n_p % tq == 0 and n_s % tq == 0 and tq % dec_seq == 0
    pos = jnp.concatenate([jnp.tile(jnp.arange(seq), batch), jnp.tile(past_len + jnp.arange(dec_seq), n_seq)])
    cs, sn = _rope_tables(pos, rope)
    cache_kpe_t = jnp.swapaxes(cache_kpe, 2, 3)

    x = jnp.concatenate([x_prompt.reshape(n_p, d_model), x_sample.reshape(n_s, d_model)], axis=0)
    outs = {k: [] for k in ("p_ckv", "p_kpe", "p_h", "p_conv", "s_ckv", "s_kpe", "s_h", "s_conv")}
    w = _prep_weights(lay, w_in, w_uq, w_uk, w_uv, w_ra, w_ri, w_o, w_gate, w_up, w_down, d_ff_pad)
    for l in range(depth):
        h = rmsnorm(x, norm_mix[l], BF16)
        z = matmul(h, w["w_in"], l, F32)
        qcat, ckv, kpe, kcat = mla_prep(z, lay, cs, sn, q_norm[l], kv_norm[l],
                                        w["wq_nope"], w["wq_pe"], w["wuk_t"], l, tq)
        dq = kcat.shape[1]
        o_attn_p = prompt_attention(qcat, kcat, w["wuv"], l, batch, seq, kv_rank, scale)
        seq_per_tile = tq // dec_seq
        q_s = qcat[n_p // tq:].reshape(n_s // tq, n_heads, seq_per_tile, dec_seq, dq)
        q_s = jnp.transpose(q_s, (0, 2, 1, 3, 4)).reshape(n_seq, n_heads * dec_seq, dq)
        k_new = kcat[n_p:].reshape(n_seq, dec_seq, dq)
        o_lat_s = sample_attention(q_s, k_new, cache_ckv, cache_kpe_t, page_table, l, scale)
        o_lat_s = jnp.transpose(o_lat_s.reshape(n_seq, n_heads, dec_seq, kv_rank), (1, 0, 2, 3))
        o_attn_s = uv_proj(o_lat_s.reshape(n_heads, n_s, kv_rank), w["wuv"], l)
        a_p, u_p = rnn_gates_prompt(z, lay, batch, seq, conv_w[l], conv_b[l], w["w_ra"][l], b_ra[l],
                                    w["w_ri"][l], b_ri[l], lru_lambda[l])
        o_rnn_p, h_last_p = rnn_scan_prompt(a_p, u_p, z, lay, batch, seq)
        conv_p = jnp.stack([z[(b + 1) * seq - (cwid - 1):(b + 1) * seq, lay["off_xr"]:lay["off_xr"] + d_rnn]
                            for b in range(batch)])
        z_s = z[n_p:].reshape(n_seq, dec_seq, -1)
        xr_s = z_s[:, :, lay["off_xr"]:lay["off_xr"] + d_rnn]
        gr_s = z_s[:, :, lay["off_gr"]:lay["off_gr"] + d_rnn]
        o_rnn_s, h_last_s = rnn_sample(jnp.swapaxes(xr_s, 0, 1), jnp.swapaxes(state_conv[l], 0, 1),
                                       jnp.swapaxes(gr_s, 0, 1), state_h[l], conv_w[l], conv_b[l],
                                       w["w_ra"][l], b_ra[l], w["w_ri"][l], b_ri[l], lru_lambda[l])
        o_rnn_s = jnp.swapaxes(o_rnn_s, 0, 1).reshape(n_s, d_rnn)
        x = matmul_res2(jnp.concatenate([o_attn_p, o_attn_s], axis=0),
                        jnp.concatenate([o_rnn_p, o_rnn_s], axis=0), w["w_o"], l, x)
        f = rmsnorm(x, norm_ffn[l], BF16)
        x = matmul_kres(swiglu(f, w["wg"], w["wu"], l), w["wd"], l, x)

        outs["p_ckv"].append(ckv[:n_p].reshape(batch, seq, kv_rank))
        outs["p_kpe"].append(kpe[:n_p].reshape(batch, seq, rope))
        outs["p_h"].append(h_last_p)
        outs["p_conv"].append(conv_p)
        outs["s_ckv"].append(ckv[n_p:].reshape(n_seq, dec_seq, kv_rank))
        outs["s_kpe"].append(kpe[n_p:].reshape(n_seq, dec_seq, rope))
        outs["s_h"].append(h_last_s)
        outs["s_conv"].append(xr_s[:, dec_seq - (cwid - 1):])

    y_prompt = rmsnorm(x, norm_final, F32, 0, n_p).reshape(batch, seq, d_model)
    y_sample = rmsnorm(x, norm_final, F32, n_p, n_s).reshape(n_seq, dec_seq, d_model)
    st = lambda k: jnp.stack(outs[k])
    return (y_prompt, y_sample, st("p_ckv"), st("p_kpe"), st("p_h"), st("p_conv"),
            st("s_ckv"), st("s_kpe"), st("s_h"), st("s_conv"))
```
